```python
import jax, jax.numpy as jnp
from jax import lax
import numpy as np

D_MODEL = 4096
BATCH = 16
SEQ = 256
DEPTH = 1
DEC_BATCH = 2
DEC_SEQ = 1024
PAST_LEN = 512

GRID_W = 64
EPS = 1e-6
ATTN_WIDTH = D_MODEL // 2
HEAD_DIM = 128
N_HEADS = ATTN_WIDTH // HEAD_DIM
N_KV_HEADS = N_HEADS // 4
KV_WIDTH = N_KV_HEADS * HEAD_DIM
POOL_WIDTH = D_MODEL - ATTN_WIDTH
POOL_WINDOWS = (2, 4, 8, 16)
N_POOL_GROUPS = len(POOL_WINDOWS)
POOL_GROUP_DIM = POOL_WIDTH // N_POOL_GROUPS
IN_WIDTH = ATTN_WIDTH + 2 * KV_WIDTH + POOL_WIDTH
Q_BLOCK = 128
ROPE_THETA = 10000.0
ROPE_PAIRS = HEAD_DIM // 4
N_KEYS = 128
N_EXPERTS = N_KEYS * N_KEYS
PEER_HEADS = 8
PEER_TOPK = 16
PEER_KEY_DIM = 512
PEER_HALF_DIM = PEER_KEY_DIM // 2
PEER_TOKEN_BLOCK = 128

kernel_name = "hybrid_gqa_pool_peer_diffusion_step"


def rms_norm(x, g):
    x32 = x.astype(jnp.float32)
    y = x32 * lax.rsqrt(jnp.mean(x32 * x32, axis=-1, keepdims=True) + EPS)
    return (y * g.astype(jnp.float32)).astype(x.dtype)


def axial_rope_angles(n_tokens):
    rows = n_tokens // GRID_W
    row_ids = jnp.repeat(jnp.arange(rows), GRID_W).astype(jnp.float32)
    col_ids = jnp.tile(jnp.arange(GRID_W), rows).astype(jnp.float32)
    inv = ROPE_THETA ** (-jnp.arange(ROPE_PAIRS, dtype=jnp.float32) / ROPE_PAIRS)
    ang_r = row_ids[:, None] * inv
    ang_c = col_ids[:, None] * inv
    return (jnp.cos(ang_r), jnp.sin(ang_r), jnp.cos(ang_c), jnp.sin(ang_c))


def rotate_half_rope(x, cos, sin):
    x32 = x.astype(jnp.float32)
    x1, x2 = x32[..., :ROPE_PAIRS], x32[..., ROPE_PAIRS:]
    cos = cos[None, :, None, :]
    sin = sin[None, :, None, :]
    return jnp.concatenate([x1 * cos - x2 * sin, x1 * sin + x2 * cos], axis=-1).astype(x.dtype)


def apply_axial_rope(x, rope):
    cr, sr, cc, sc = rope
    half = HEAD_DIM // 2
    return jnp.concatenate([rotate_half_rope(x[..., :half], cr, sr),
                            rotate_half_rope(x[..., half:], cc, sc)], axis=-1)


def blocked_attention(q, k, v):
    B, Tq, H, hd = q.shape
    G = H // N_KV_HEADS
    qb = jnp.moveaxis(q.reshape(B, Tq // Q_BLOCK, Q_BLOCK, N_KV_HEADS, G, hd), 1, 0)
    scale = hd ** -0.5

    def one_block(qblk):
        s = jnp.einsum('bqkgd,bskd->bkgqs', qblk, k, preferred_element_type=jnp.float32) * scale
        p = jax.nn.softmax(s, axis=-1).astype(v.dtype)
        return jnp.einsum('bkgqs,bskd->bqkgd', p, v)

    out = lax.map(one_block, qb)
    return jnp.moveaxis(out, 0, 1).reshape(B, Tq, H * hd)


def multiscale_pool(xp, w_pool, pool_scale):
    B, T, _ = xp.shape
    x32 = xp.astype(jnp.float32)
    cs = jnp.concatenate([jnp.zeros((B, 1, POOL_WIDTH), jnp.float32), lax.cumsum(x32, axis=1)], axis=1)
    t = jnp.arange(T)
    outs = []
    for g, w in enumerate(POOL_WINDOWS):
        lo = jnp.clip(t - w // 2, 0, T)
        hi = jnp.clip(t - w // 2 + w, 0, T)
        csg = cs[..., g * POOL_GROUP_DIM:(g + 1) * POOL_GROUP_DIM]
        cnt = (hi - lo).astype(jnp.float32)[None, :, None]
        mean = (csg[:, hi] - csg[:, lo]) / cnt
        outs.append(mean - x32[..., g * POOL_GROUP_DIM:(g + 1) * POOL_GROUP_DIM])
    pooled = jnp.stack(outs, axis=2).astype(xp.dtype)
    mixed = jnp.einsum('btgc,gcd->btgd', pooled, w_pool)
    return mixed.reshape(B, T, POOL_WIDTH) * pool_scale


def peer_ffn(h, w_query, sub_keys, expert_u, expert_v):
    B, T, D = h.shape
    n = B * T
    x = h.reshape(n, D)
    q = (x @ w_query).reshape(n, PEER_HEADS, 2, PEER_HALF_DIM)
    s = jnp.einsum('nhpc,hpkc->nhpk', q, sub_keys, preferred_element_type=jnp.float32)
    s1, i1 = lax.top_k(s[:, :, 0], PEER_TOPK)
    s2, i2 = lax.top_k(s[:, :, 1], PEER_TOPK)
    cand_s = (s1[..., :, None] + s2[..., None, :]).reshape(n, PEER_HEADS, PEER_TOPK * PEER_TOPK)
    cand_i = (i1[..., :, None] * N_KEYS + i2[..., None, :]).reshape(n, PEER_HEADS, PEER_TOPK * PEER_TOPK)
    top_s, pos = lax.top_k(cand_s, PEER_TOPK)
    idx = jnp.take_along_axis(cand_i, pos, axis=-1)
    gate = jax.nn.softmax(top_s, axis=-1)
    nb = n // PEER_TOKEN_BLOCK

    def one_block(args):
        xb, ib, gb = args
        u = jnp.take(expert_u, ib, axis=0)
        a = jax.nn.gelu(jnp.einsum('td,thkd->thk', xb, u, preferred_element_type=jnp.float32))
        wgt = (gb * a).astype(xb.dtype)
        v = jnp.take(expert_v, ib, axis=0)
        return jnp.einsum('thk,thkd->td', wgt, v)

    out = lax.map(one_block, (x.reshape(nb, PEER_TOKEN_BLOCK, D),
                              idx.reshape(nb, PEER_TOKEN_BLOCK, PEER_HEADS, PEER_TOPK),
                              gate.reshape(nb, PEER_TOKEN_BLOCK, PEER_HEADS, PEER_TOPK)))
    return out.reshape(B, T, D)


def trunk_layer(x, cond, ctx_k, ctx_v, rope, w_mod, b_mod, pre_mix_g, post_mix_g, pre_ffn_g, post_ffn_g,
                w_in, q_norm_g, k_norm_g, w_pool, pool_scale, w_out, peer_w_query, peer_sub_keys,
                peer_u, peer_v):
    B, T, _ = x.shape
    mods = (jax.nn.silu(cond) @ w_mod + b_mod)[:, None, :]
    sh1, sc1, g1, sh2, sc2, g2 = jnp.split(mods, 6, axis=-1)
    h = rms_norm(x, pre_mix_g) * (1 + sc1) + sh1
    proj = h @ w_in
    q = proj[..., :ATTN_WIDTH].reshape(B, T, N_HEADS, HEAD_DIM)
    k = proj[..., ATTN_WIDTH:ATTN_WIDTH + KV_WIDTH].reshape(B, T, N_KV_HEADS, HEAD_DIM)
    v = proj[..., ATTN_WIDTH + KV_WIDTH:ATTN_WIDTH + 2 * KV_WIDTH].reshape(B, T, N_KV_HEADS, HEAD_DIM)
    xp = proj[..., ATTN_WIDTH + 2 * KV_WIDTH:]
    q = rms_norm(q, q_norm_g)
    k = rms_norm(k, k_norm_g)
    if rope is None:
        k_all, v_all = k, v
    else:
        q = apply_axial_rope(q, rope)
        k = apply_axial_rope(k, rope)
        k_all = jnp.concatenate([k, ctx_k], axis=1)
        v_all = jnp.concatenate([v, ctx_v], axis=1)
    attn = blocked_attention(q, k_all, v_all)
    pool = multiscale_pool(xp, w_pool, pool_scale)
    mix = jnp.concatenate([attn, pool], axis=-1) @ w_out
    x = x + g1 * rms_norm(mix, post_mix_g)
    h = rms_norm(x, pre_ffn_g) * (1 + sc2) + sh2
    f = peer_ffn(h, peer_w_query, peer_sub_keys, peer_u, peer_v)
    x = x + g2 * rms_norm(f, post_ffn_g)
    return x, k, v


def setup_inputs(seed: int = 0) -> dict:
    key = jax.random.key(seed)
    ks = jax.random.split(key, 24)
    D = D_MODEL

    def nrm(k, shape, scale):
        return jax.random.normal(k, shape, jnp.float32) * scale

    return {
        "x_prompt": nrm(ks[0], (BATCH, SEQ, D), 1.0),
        "x_sample": nrm(ks[1], (DEC_BATCH, DEC_SEQ, D), 1.0),
        "cache_k": nrm(ks[2], (DEC_BATCH, DEPTH, PAST_LEN, N_KV_HEADS, HEAD_DIM), 1.0),
        "cache_v": nrm(ks[3], (DEC_BATCH, DEPTH, PAST_LEN, N_KV_HEADS, HEAD_DIM), 1.0),
        "c": nrm(ks[4], (DEC_BATCH, D), 1.0),
        "c_ctx": nrm(ks[5], (D,), 1.0),
        "w_mod": nrm(ks[6], (DEPTH, D, 6 * D), 0.5 * D ** -0.5),
        "b_mod": nrm(ks[7], (DEPTH, 6 * D), 0.02),
        "pre_mix_g": 1.0 + nrm(ks[8], (DEPTH, D), 0.05),
        "post_mix_g": 1.0 + nrm(ks[9], (DEPTH, D), 0.05),
        "pre_ffn_g": 1.0 + nrm(ks[10], (DEPTH, D), 0.05),
        "post_ffn_g": 1.0 + nrm(ks[11], (DEPTH, D), 0.05),
        "w_in": nrm(ks[12], (DEPTH, D, IN_WIDTH), D ** -0.5),
        "q_norm_g": 1.0 + nrm(ks[13], (DEPTH, HEAD_DIM), 0.05),
        "k_norm_g": 1.0 + nrm(ks[14], (DEPTH, HEAD_DIM), 0.05),
        "w_pool": nrm(ks[15], (DEPTH, N_POOL_GROUPS, POOL_GROUP_DIM, POOL_GROUP_DIM), POOL_GROUP_DIM ** -0.5),
        "pool_scale": 1.0 + nrm(ks[16], (DEPTH, POOL_WIDTH), 0.1),
        "w_out": nrm(ks[17], (DEPTH, D, D), D ** -0.5),
        "peer_w_query": nrm(ks[18], (DEPTH, D, PEER_HEADS * PEER_KEY_DIM), D ** -0.5),
        "peer_sub_keys": nrm(ks[19], (DEPTH, PEER_HEADS, 2, N_KEYS, PEER_HALF_DIM), PEER_HALF_DIM ** -0.5),
        "peer_u": nrm(ks[20], (DEPTH, N_EXPERTS, D), D ** -0.5),
        "peer_v": nrm(ks[21], (DEPTH, N_EXPERTS, D), D ** -0.5),
    }


def reference(x_prompt, x_sample, cache_k, cache_v, c, c_ctx, w_mod, b_mod, pre_mix_g, post_mix_g,
              pre_ffn_g, post_ffn_g, w_in, q_norm_g, k_norm_g, w_pool, pool_scale, w_out,
              peer_w_query, peer_sub_keys, peer_u, peer_v):
    rope = axial_rope_angles(x_sample.shape[1])
    y_prompt, y_sample = x_prompt, x_sample
    new_k, new_v = [], []
    for l in range(DEPTH):
        lw = [p[l] for p in (w_mod, b_mod, pre_mix_g, post_mix_g, pre_ffn_g, post_ffn_g, w_in,
                             q_norm_g, k_norm_g, w_pool, pool_scale, w_out, peer_w_query,
                             peer_sub_keys, peer_u, peer_v)]
        y_prompt, k_l, v_l = trunk_layer(y_prompt, c_ctx[None, :], None, None, None, *lw)
        new_k.append(k_l)
        new_v.append(v_l)
        y_sample, _, _ = trunk_layer(y_sample, c, cache_k[:, l], cache_v[:, l], rope, *lw)
    state_k = jnp.stack(new_k, axis=1)
    state_v = jnp.stack(new_v, axis=1)
    return (y_prompt, y_sample, state_k, state_v)
```

```python
import functools
import math

import jax
import jax.numpy as jnp
from jax import lax
from jax.experimental import pallas as pl
from jax.experimental.pallas import tpu as pltpu

F32 = jnp.float32
BF16 = jnp.bfloat16

EPS = 1e-6
GRID_W = 64
ROPE_THETA = 10000.0
KV_GROUP = 4
POOL_WINDOWS = (2, 4, 8, 16)
PEER_TOPK = 16
LANES = 128
BF16_ROWS = 16
GM_PITCH = 136
VMEM_LIMIT = 56 * 1024 * 1024


def _params(semantics):
    return pltpu.CompilerParams(dimension_semantics=semantics, vmem_limit_bytes=VMEM_LIMIT)


def _dot(a, b):
    return jnp.dot(a, b, preferred_element_type=F32)


def _dot_nt(a, b):
    return lax.dot_general(a, b, (((1,), (1,)), ((), ())), preferred_element_type=F32)


def _rms(x, g):
    return x * lax.rsqrt(jnp.mean(x * x, axis=-1, keepdims=True) + EPS) * g


def _tile(n, pref):
    return pref if n % pref == 0 else n


def _mods_map(mods, tiles_per_seq):
    if mods.shape[0] == 1:
        return lambda i: (0, 0, 0)
    return lambda i: (i // tiles_per_seq, 0, 0)


def _mods_kernel(c_ref, w_ref, b_ref, o_ref):
    c = c_ref[...]
    s = (c * jax.nn.sigmoid(c)).astype(BF16)
    o_ref[...] = _dot(s, w_ref[...].astype(BF16)) + b_ref[...]


def _mods(cond, w_mod, b_mod):
    r, d = cond.shape
    n = w_mod.shape[1]
    rp = 8
    cond_p = jnp.zeros((rp, d), F32).at[:r].set(cond)
    tn = _tile(n, 512)
    out = pl.pallas_call(
        _mods_kernel,
        grid=(n // tn,),
        in_specs=[pl.BlockSpec((rp, d), lambda j: (0, 0)),
                  pl.BlockSpec((d, tn), lambda j: (0, j)),
                  pl.BlockSpec((1, tn), lambda j: (0, j))],
        out_specs=pl.BlockSpec((rp, tn), lambda j: (0, j)),
        out_shape=jax.ShapeDtypeStruct((rp, n), F32),
        compiler_params=_params(("arbitrary",)),
        name="mods",
    )(cond_p, w_mod, b_mod.reshape(1, n))
    return out[:r].reshape(r, 6, d)


def _prenorm_kernel(x_ref, g_ref, m_ref, o_ref):
    y = _rms(x_ref[...], g_ref[...])
    o_ref[...] = (y * (1.0 + m_ref[0, 1:2, :]) + m_ref[0, 0:1, :]).astype(o_ref.dtype)


def _prenorm(x, g, mods, seq):
    n, d = x.shape
    tm = _tile(seq, 256)
    mods_map = _mods_map(mods, seq // tm)
    return pl.pallas_call(
        _prenorm_kernel,
        grid=(n // tm,),
        in_specs=[pl.BlockSpec((tm, d), lambda i: (i, 0)),
                  pl.BlockSpec((1, d), lambda i: (0, 0)),
                  pl.BlockSpec((1, 6, d), mods_map)],
        out_specs=pl.BlockSpec((tm, d), lambda i: (i, 0)),
        out_shape=jax.ShapeDtypeStruct((n, d), BF16),
        compiler_params=_params(("arbitrary",)),
        name="prenorm",
    )(x, g.reshape(1, d), mods)


def _rope(y, cos, sin):
    lane = lax.broadcasted_iota(jnp.int32, y.shape, 1)
    quarter = LANES // 4
    swapped = jnp.where((lane & quarter) == 0,
                        pltpu.roll(y, LANES - quarter, axis=1),
                        pltpu.roll(y, quarter, axis=1))
    return y * cos + swapped * sin


def _proj_kernel(*refs, head_norm, rope):
    a_ref, w_ref = refs[0], refs[1]
    o_ref = refs[-1]
    acc = _dot(a_ref[...], w_ref[...].astype(BF16))
    if not head_norm:
        o_ref[...] = acc.astype(o_ref.dtype)
        return
    g = refs[2][...]
    if rope:
        cos, sin = refs[3][...], refs[4][...]
    for h in range(acc.shape[1] // LANES):
        cols = slice(h * LANES, (h + 1) * LANES)
        y = _rms(acc[:, cols], g)
        if rope:
            y = _rope(y, cos, sin)
        o_ref[:, cols] = y.astype(o_ref.dtype)


def _proj(a, w, col0, ncols, out_dtype, head_g=None, rope=None, seq=None):
    n, k = a.shape
    tm = _tile(n if rope is None else seq, 512)
    tn = math.gcd(math.gcd(ncols, col0), 512)
    assert tn % LANES == 0
    jb = col0 // tn
    in_specs = [pl.BlockSpec((tm, k), lambda j, i: (i, 0)),
                pl.BlockSpec((k, tn), lambda j, i: (0, jb + j))]
    args = [a, w]
    if head_g is not None:
        in_specs.append(pl.BlockSpec((1, LANES), lambda j, i: (0, 0)))
        args.append(head_g.reshape(1, LANES))
    if rope is not None:
        per = seq // tm
        for t in rope:
            in_specs.append(pl.BlockSpec((tm, LANES), lambda j, i: (i % per, 0)))
            args.append(t)
    return pl.pallas_call(
        functools.partial(_proj_kernel, head_norm=head_g is not None, rope=rope is not None),
        grid=(ncols // tn, n // tm),
        in_specs=in_specs,
        out_specs=pl.BlockSpec((tm, tn), lambda j, i: (i, j)),
        out_shape=jax.ShapeDtypeStruct((n, ncols), out_dtype),
        compiler_params=_params(("arbitrary", "arbitrary")),
        name="proj",
    )(*args)


def _attn_kernel(*refs, scale, has_cache):
    q_ref, k_ref, v_ref = refs[:3]
    o_ref = refs[-1]
    k = k_ref[...].astype(BF16)
    v = v_ref[...].astype(BF16)
    if has_cache:
        ck = refs[3][...].astype(BF16)
        cv = refs[4][...].astype(BF16)
    for g in range(KV_GROUP):
        cols = slice(g * LANES, (g + 1) * LANES)
        qh = q_ref[:, cols]
        s = _dot_nt(qh, k) * scale
        m = jnp.max(s, axis=-1, keepdims=True)
        if has_cache:
            s2 = _dot_nt(qh, ck) * scale
            m = jnp.maximum(m, jnp.max(s2, axis=-1, keepdims=True))
        p = jnp.exp(s - m)
        l = jnp.sum(p, axis=-1, keepdims=True)
        if has_cache:
            p2 = jnp.exp(s2 - m)
            l = l + jnp.sum(p2, axis=-1, keepdims=True)
        inv = 1.0 / l
        o = _dot((p * inv).astype(BF16), v)
        if has_cache:
            o = o + _dot((p2 * inv).astype(BF16), cv)
        o_ref[:, cols] = o.astype(o_ref.dtype)


def _attention(q, k, v, batch, seq, cache=None):
    n, qw = q.shape
    kvh = k.shape[1] // LANES
    gw = KV_GROUP * LANES
    tq = _tile(seq, 256)
    per = seq // tq
    in_specs = [pl.BlockSpec((tq, gw), lambda b, h, i: (b * per + i, h)),
                pl.BlockSpec((seq, LANES), lambda b, h, i: (b, h)),
                pl.BlockSpec((seq, LANES), lambda b, h, i: (b, h))]
    args = [q, k, v]
    if cache is not None:
        past = cache[0].shape[1]
        for c in cache:
            in_specs.append(pl.BlockSpec((None, past, LANES), lambda b, h, i: (b, 0, h)))
            args.append(c)
    return pl.pallas_call(
        functools.partial(_attn_kernel, scale=LANES ** -0.5, has_cache=cache is not None),
        grid=(batch, kvh, per),
        in_specs=in_specs,
        out_specs=pl.BlockSpec((tq, gw), lambda b, h, i: (b * per + i, h)),
        out_shape=jax.ShapeDtypeStruct((n, qw), BF16),
        compiler_params=_params(("arbitrary", "arbitrary", "arbitrary")),
        name="attention",
    )(*args)


def _pool_kernel(x_ref, w_ref, s_ref, o_ref):
    g = pl.program_id(1)
    x = x_ref[...]
    t = x.shape[0]
    row = lax.broadcasted_iota(jnp.int32, x.shape, 0)
    for gi, win in enumerate(POOL_WINDOWS):
        @pl.when(g == gi)
        def _():
            half = win // 2
            acc = x
            for d in range(-half, half):
                if d == 0:
                    continue
                shifted = pltpu.roll(x, (-d) % t, axis=0)
                valid = (row >= -d) if d < 0 else (row < t - d)
                acc = acc + jnp.where(valid, shifted, 0.0)
            cnt = (jnp.minimum(row + half, t) - jnp.maximum(row - half, 0)).astype(F32)
            pooled = acc / cnt - x
            mixed = _dot(pooled.astype(BF16), w_ref[...].astype(BF16)) * s_ref[...]
            o_ref[...] = mixed.astype(o_ref.dtype)


def _pool(xp, w_pool, pool_scale, batch, seq):
    n, pw = xp.shape
    ng, gd, _ = w_pool.shape
    return pl.pallas_call(
        _pool_kernel,
        grid=(batch, ng),
        in_specs=[pl.BlockSpec((seq, gd), lambda b, g: (b, g)),
                  pl.BlockSpec((None, gd, gd), lambda b, g: (g, 0, 0)),
                  pl.BlockSpec((1, gd), lambda b, g: (0, g))],
        out_specs=pl.BlockSpec((seq, gd), lambda b, g: (b, g)),
        out_shape=jax.ShapeDtypeStruct((n, pw), BF16),
        compiler_params=_params(("arbitrary", "arbitrary")),
        name="pool",
    )(xp, w_pool, pool_scale.reshape(1, pw))


def _outproj_kernel(a_ref, p_ref, wt_ref, wb_ref, o_ref):
    o_ref[...] = (_dot(a_ref[...], wt_ref[...].astype(BF16))
                  + _dot(p_ref[...], wb_ref[...].astype(BF16)))


def _outproj(attn, pool, w_out):
    n, aw = attn.shape
    pw = pool.shape[1]
    assert aw == pw
    d = w_out.shape[1]
    tm = _tile(n, 512)
    tn = _tile(d, 512)
    return pl.pallas_call(
        _outproj_kernel,
        grid=(d // tn, n // tm),
        in_specs=[pl.BlockSpec((tm, aw), lambda j, i: (i, 0)),
                  pl.BlockSpec((tm, pw), lambda j, i: (i, 0)),
                  pl.BlockSpec((aw, tn), lambda j, i: (0, j)),
                  pl.BlockSpec((pw, tn), lambda j, i: (1, j))],
        out_specs=pl.BlockSpec((tm, tn), lambda j, i: (i, j)),
        out_shape=jax.ShapeDtypeStruct((n, d), F32),
        compiler_params=_params(("arbitrary", "arbitrary")),
        name="outproj",
    )(attn, pool, w_out, w_out)


def _postmix_kernel(x_ref, mix_ref, gpost_ref, gpre_ref, m_ref, x1_ref, h2_ref):
    x1 = x_ref[...] + m_ref[0, 2:3, :] * _rms(mix_ref[...], gpost_ref[...])
    x1_ref[...] = x1
    y = _rms(x1, gpre_ref[...])
    h2_ref[...] = (y * (1.0 + m_ref[0, 4:5, :]) + m_ref[0, 3:4, :]).astype(h2_ref.dtype)


def _postmix(x, mix, g_post, g_pre, mods, seq):
    n, d = x.shape
    tm = _tile(seq, 256)
    row = pl.BlockSpec((tm, d), lambda i: (i, 0))
    vec = pl.BlockSpec((1, d), lambda i: (0, 0))
    return pl.pallas_call(
        _postmix_kernel,
        grid=(n // tm,),
        in_specs=[row, row, vec, vec, pl.BlockSpec((1, 6, d), _mods_map(mods, seq // tm))],
        out_specs=[row, row],
        out_shape=[jax.ShapeDtypeStruct((n, d), F32), jax.ShapeDtypeStruct((n, d), BF16)],
        compiler_params=_params(("arbitrary",)),
        name="postmix",
    )(x, mix, g_post.reshape(1, d), g_pre.reshape(1, d), mods)


def _top16(s):
    nrows, width = s.shape
    row = lax.broadcasted_iota(jnp.int32, s.shape, 0)
    slot = lax.broadcasted_iota(jnp.int32, (PEER_TOPK, width), 0)
    vals = jnp.zeros((PEER_TOPK, width), F32)
    idxs = jnp.zeros((PEER_TOPK, width), jnp.int32)
    for k in range(PEER_TOPK):
        m = jnp.max(s, axis=0, keepdims=True)
        idx = jnp.min(jnp.where(s == m, row, nrows), axis=0, keepdims=True)
        vals = jnp.where(slot == k, m, vals)
        idxs = jnp.where(slot == k, idx, idxs)
        s = jnp.where(row == idx, -jnp.inf, s)
    return vals, idxs


def _topk_kernel(q_ref, sk_ref, i_ref, j_ref, g_ref):
    half = sk_ref.shape[-1]
    v1, i1 = _top16(_dot_nt(sk_ref[0].astype(BF16), q_ref[:, :half]))
    v2, i2 = _top16(_dot_nt(sk_ref[1].astype(BF16), q_ref[:, half:]))
    cand = jnp.concatenate([v1[a:a + 1, :] + v2 for a in range(PEER_TOPK)], axis=0)
    tv, pos = _top16(cand)
    a_sel = pos // PEER_TOPK
    b_sel = pos % PEER_TOPK
    isel = jnp.zeros_like(pos)
    jsel = jnp.zeros_like(pos)
    for a in range(PEER_TOPK):
        isel = jnp.where(a_sel == a, i1[a:a + 1, :], isel)
        jsel = jnp.where(b_sel == a, i2[a:a + 1, :], jsel)
    e = jnp.exp(tv - tv[0:1, :])
    i_ref[...] = isel
    j_ref[...] = jsel
    g_ref[...] = e / jnp.sum(e, axis=0, keepdims=True)


def _peer_topk(q, sub_keys):
    n = q.shape[0]
    heads, _, nkeys, half = sub_keys.shape
    tt = _tile(n, 256)
    out = pl.BlockSpec((None, PEER_TOPK, tt), lambda i, h: (h, 0, i))
    return pl.pallas_call(
        _topk_kernel,
        grid=(n // tt, heads),
        in_specs=[pl.BlockSpec((tt, 2 * half), lambda i, h: (i, h)),
                  pl.BlockSpec((None, 2, nkeys, half), lambda i, h: (h, 0, 0, 0))],
        out_specs=[out, out, out],
        out_shape=[jax.ShapeDtypeStruct((heads, PEER_TOPK, n), jnp.int32),
                   jax.ShapeDtypeStruct((heads, PEER_TOPK, n), jnp.int32),
                   jax.ShapeDtypeStruct((heads, PEER_TOPK, n), F32)],
        compiler_params=_params(("arbitrary", "arbitrary")),
        name="peer_topk",
    )(q, sub_keys)


def _gate_matrix_kernel(i_ref, j_ref, g_ref, o_ref, t_ref, *, nkeys):
    tile = i_ref.shape[0]
    sub = lax.broadcasted_iota(jnp.int32, (nkeys, i_ref.shape[1]), 0)

    def group(gidx, carry):
        base = pl.multiple_of(gidx * BF16_ROWS, BF16_ROWS)
        for r in range(BF16_ROWS):
            irow = i_ref[pl.ds(base + r, 1), :]
            jrow = j_ref[pl.ds(base + r, 1), :]
            grow = g_ref[pl.ds(base + r, 1), :]
            a = jnp.where(sub == irow, grow, 0.0).astype(BF16)
            b = jnp.where(sub == jrow, 1.0, 0.0).astype(BF16)
            t_ref[r * GM_PITCH:r * GM_PITCH + nkeys, :] = _dot_nt(a, b)
        for i in range(nkeys):
            rows = t_ref[pl.ds(i, BF16_ROWS, stride=GM_PITCH), :]
            o_ref[pl.ds(base, BF16_ROWS), i * nkeys:(i + 1) * nkeys] = rows.astype(o_ref.dtype)
        return carry

    lax.fori_loop(0, tile // BF16_ROWS, group, 0)


def _gate_matrix(isel, jsel, gate, nkeys):
    n, s = isel.shape
    assert nkeys == LANES
    tile = _tile(n, 128)
    row = pl.BlockSpec((tile, s), lambda i: (i, 0))
    return pl.pallas_call(
        functools.partial(_gate_matrix_kernel, nkeys=nkeys),
        grid=(n // tile,),
        in_specs=[row, row, row],
        out_specs=pl.BlockSpec((tile, nkeys * nkeys), lambda i: (i, 0)),
        out_shape=jax.ShapeDtypeStruct((n, nkeys * nkeys), BF16),
        scratch_shapes=[pltpu.VMEM((BF16_ROWS * GM_PITCH, nkeys), F32)],
        compiler_params=_params(("arbitrary",)),
        name="gate_matrix",
    )(isel, jsel, gate)


def _gelu_tanh(x):
    return x * (0.5 * (1.0 + jnp.tanh(math.sqrt(2.0 / math.pi) * (x + 0.044715 * (x * x * x)))))


def _expert_kernel(h_ref, gm_ref, u_ref, v_ref, o_ref):
    e = pl.program_id(1)
    act = _gelu_tanh(_dot_nt(h_ref[...], u_ref[...]))
    w = (gm_ref[...].astype(F32) * act).astype(BF16)
    contrib = _dot(w, v_ref[...])

    @pl.when(e == 0)
    def _():
        o_ref[...] = contrib

    @pl.when(e > 0)
    def _():
        o_ref[...] += contrib


def _experts(h, gm, u, v):
    n, d = h.shape
    ne = u.shape[0]
    tb = _tile(n, 512)
    te = _tile(ne, 512)
    return pl.pallas_call(
        _expert_kernel,
        grid=(n // tb, ne // te),
        in_specs=[pl.BlockSpec((tb, d), lambda i, e: (i, 0)),
                  pl.BlockSpec((tb, te), lambda i, e: (i, e)),
                  pl.BlockSpec((te, d), lambda i, e: (e, 0)),
                  pl.BlockSpec((te, d), lambda i, e: (e, 0))],
        out_specs=pl.BlockSpec((tb, d), lambda i, e: (i, 0)),
        out_shape=jax.ShapeDtypeStruct((n, d), F32),
        compiler_params=_params(("arbitrary", "arbitrary")),
        name="experts",
    )(h, gm, u, v)


def _final_kernel(x_ref, f_ref, g_ref, m_ref, o_ref):
    o_ref[...] = x_ref[...] + m_ref[0, 5:6, :] * _rms(f_ref[...], g_ref[...])


def _final(x1, f, g_post, mods, seq):
    n, d = x1.shape
    tm = _tile(seq, 256)
    row = pl.BlockSpec((tm, d), lambda i: (i, 0))
    return pl.pallas_call(
        _final_kernel,
        grid=(n // tm,),
        in_specs=[row, row, pl.BlockSpec((1, d), lambda i: (0, 0)),
                  pl.BlockSpec((1, 6, d), _mods_map(mods, seq // tm))],
        out_specs=row,
        out_shape=jax.ShapeDtypeStruct((n, d), F32),
        compiler_params=_params(("arbitrary",)),
        name="final",
    )(x1, f, g_post.reshape(1, d), mods)


def _rope_tables(seq):
    pairs = LANES // 4
    t = jnp.arange(seq)
    inv = ROPE_THETA ** (-jnp.arange(pairs, dtype=F32) / pairs)
    ang_r = (t // GRID_W).astype(F32)[:, None] * inv
    ang_c = (t % GRID_W).astype(F32)[:, None] * inv
    cos = jnp.concatenate([jnp.cos(ang_r)] * 2 + [jnp.cos(ang_c)] * 2, axis=-1)
    sin = jnp.concatenate([-jnp.sin(ang_r), jnp.sin(ang_r), -jnp.sin(ang_c), jnp.sin(ang_c)], axis=-1)
    return cos, sin


def _trunk(x, mods, cache, rope, lw):
    (pre_mix_g, post_mix_g, pre_ffn_g, post_ffn_g, w_in, q_norm_g, k_norm_g, w_pool, pool_scale,
     w_out, peer_w_query, peer_sub_keys, peer_u, peer_v) = lw
    batch, seq, d = x.shape
    n = batch * seq
    x2 = x.reshape(n, d)
    attn_w = d // 2
    kv_w = attn_w // KV_GROUP

    h = _prenorm(x2, pre_mix_g, mods, seq)
    q = _proj(h, w_in, 0, attn_w, BF16, head_g=q_norm_g, rope=rope, seq=seq)
    k = _proj(h, w_in, attn_w, kv_w, F32, head_g=k_norm_g, rope=rope, seq=seq)
    v = _proj(h, w_in, attn_w + kv_w, kv_w, F32)
    xp = _proj(h, w_in, attn_w + 2 * kv_w, d - attn_w, F32)
    attn = _attention(q, k, v, batch, seq, cache)
    pool = _pool(xp, w_pool, pool_scale, batch, seq)
    mix = _outproj(attn, pool, w_out)
    x1, h2 = _postmix(x2, mix, post_mix_g, pre_ffn_g, mods, seq)

    heads, _, nkeys, _ = peer_sub_keys.shape
    pq = _proj(h2, peer_w_query, 0, peer_w_query.shape[1], BF16)
    isel, jsel, gate = _peer_topk(pq, peer_sub_keys)
    flat = lambda a: a.reshape(heads * PEER_TOPK, n).T
    gm = _gate_matrix(flat(isel), flat(jsel), flat(gate), nkeys)
    f = _experts(h2, gm, peer_u, peer_v)
    y = _final(x1, f, post_ffn_g, mods, seq)
    return y.reshape(batch, seq, d), k, v


def kernel(x_prompt, x_sample, cache_k, cache_v, c, c_ctx, w_mod, b_mod, pre_mix_g, post_mix_g,
           pre_ffn_g, post_ffn_g, w_in, q_norm_g, k_norm_g, w_pool, pool_scale, w_out,
           peer_w_query, peer_sub_keys, peer_u, peer_v):
    depth = w_mod.shape[0]
    dec_batch, dec_seq, _ = x_sample.shape
    batch, seq, _ = x_prompt.shape
    kvh, hd = cache_k.shape[3], cache_k.shape[4]
    rope = _rope_tables(dec_seq)
    cond = jnp.concatenate([c_ctx[None, :], c], axis=0)
    y_prompt, y_sample = x_prompt, x_sample
    new_k, new_v = [], []
    for l in range(depth):
        mods = _mods(cond, w_mod[l], b_mod[l])
        lw = [p[l] for p in (pre_mix_g, post_mix_g, pre_ffn_g, post_ffn_g, w_in, q_norm_g, k_norm_g,
                             w_pool, pool_scale, w_out, peer_w_query, peer_sub_keys)]
        lw += [peer_u[l].astype(BF16), peer_v[l].astype(BF16)]
        y_prompt, k_l, v_l = _trunk(y_prompt, mods[:1], None, None, lw)
        new_k.append(k_l.reshape(batch, seq, kvh, hd))
        new_v.append(v_l.reshape(batch, seq, kvh, hd))
        cache = (cache_k[:, l].reshape(dec_batch, -1, kvh * hd),
                 cache_v[:, l].reshape(dec_batch, -1, kvh * hd))
        y_sample, _, _ = _trunk(y_sample, mods[1:], cache, rope, lw)
    return (y_prompt, y_sample, jnp.stack(new_k, axis=1), jnp.stack(new_v, axis=1))
```

```python
import functools
import math

import jax
import jax.numpy as jnp
from jax import lax
from jax.experimental import pallas as pl
from jax.experimental.pallas import tpu as pltpu

F32 = jnp.float32
BF16 = jnp.bfloat16

EPS = 1e-6
GRID_W = 64
ROPE_THETA = 10000.0
KV_GROUP = 4
POOL_WINDOWS = (2, 4, 8, 16)
PEER_TOPK = 16
LANES = 128
BF16_ROWS = 16
GM_PITCH = 136
EXPERT_SUBTILE = 256
EXPERT_OUT_CHUNK = 512
VMEM_LIMIT = 56 * 1024 * 1024


def _params(semantics):
    return pltpu.CompilerParams(dimension_semantics=semantics, vmem_limit_bytes=VMEM_LIMIT)


def _dot(a, b):
    return jnp.dot(a, b, preferred_element_type=F32)


def _dot_nt(a, b):
    return lax.dot_general(a, b, (((1,), (1,)), ((), ())), preferred_element_type=F32)


def _rms(x, g):
    return x * lax.rsqrt(jnp.mean(x * x, axis=-1, keepdims=True) + EPS) * g


def _tile(n, pref):
    return pref if n % pref == 0 else n


def _mods_map(mods, tiles_per_seq):
    if mods.shape[0] == 1:
        return lambda i: (0, 0, 0)
    return lambda i: (i // tiles_per_seq, 0, 0)


def _mods_kernel(c_ref, w_ref, b_ref, o_ref):
    c = c_ref[...]
    s = (c * jax.nn.sigmoid(c)).astype(BF16)
    o_ref[...] = _dot(s, w_ref[...].astype(BF16)) + b_ref[...]


def _mods(cond, w_mod, b_mod):
    r, d = cond.shape
    n = w_mod.shape[1]
    rp = 8
    cond_p = jnp.zeros((rp, d), F32).at[:r].set(cond)
    tn = _tile(n, 512)
    out = pl.pallas_call(
        _mods_kernel,
        grid=(n // tn,),
        in_specs=[pl.BlockSpec((rp, d), lambda j: (0, 0)),
                  pl.BlockSpec((d, tn), lambda j: (0, j)),
                  pl.BlockSpec((1, tn), lambda j: (0, j))],
        out_specs=pl.BlockSpec((rp, tn), lambda j: (0, j)),
        out_shape=jax.ShapeDtypeStruct((rp, n), F32),
        compiler_params=_params(("arbitrary",)),
        name="mods",
    )(cond_p, w_mod, b_mod.reshape(1, n))
    return out[:r].reshape(r, 6, d)


def _prenorm_kernel(x_ref, g_ref, m_ref, o_ref):
    y = _rms(x_ref[...], g_ref[...])
    o_ref[...] = (y * (1.0 + m_ref[0, 1:2, :]) + m_ref[0, 0:1, :]).astype(o_ref.dtype)


def _prenorm(x, g, mods, seq):
    n, d = x.shape
    tm = _tile(seq, 256)
    mods_map = _mods_map(mods, seq // tm)
    return pl.pallas_call(
        _prenorm_kernel,
        grid=(n // tm,),
        in_specs=[pl.BlockSpec((tm, d), lambda i: (i, 0)),
                  pl.BlockSpec((1, d), lambda i: (0, 0)),
                  pl.BlockSpec((1, 6, d), mods_map)],
        out_specs=pl.BlockSpec((tm, d), lambda i: (i, 0)),
        out_shape=jax.ShapeDtypeStruct((n, d), BF16),
        compiler_params=_params(("arbitrary",)),
        name="prenorm",
    )(x, g.reshape(1, d), mods)


def _rope(y, cos, sin):
    lane = lax.broadcasted_iota(jnp.int32, y.shape, 1)
    quarter = LANES // 4
    swapped = jnp.where((lane & quarter) == 0,
                        pltpu.roll(y, LANES - quarter, axis=1),
                        pltpu.roll(y, quarter, axis=1))
    return y * cos + swapped * sin


def _proj_kernel(*refs, head_norm, rope):
    a_ref, w_ref = refs[0], refs[1]
    o_ref = refs[-1]
    acc = _dot(a_ref[...], w_ref[...].astype(BF16))
    if not head_norm:
        o_ref[...] = acc.astype(o_ref.dtype)
        return
    g = refs[2][...]
    if rope:
        cos, sin = refs[3][...], refs[4][...]
    for h in range(acc.shape[1] // LANES):
        cols = slice(h * LANES, (h + 1) * LANES)
        y = _rms(acc[:, cols], g)
        if rope:
            y = _rope(y, cos, sin)
        o_ref[:, cols] = y.astype(o_ref.dtype)


def _proj(a, w, col0, ncols, out_dtype, head_g=None, rope=None, seq=None):
    n, k = a.shape
    tm = _tile(n if rope is None else seq, 512)
    tn = math.gcd(math.gcd(ncols, col0), 512)
    assert tn % LANES == 0
    jb = col0 // tn
    in_specs = [pl.BlockSpec((tm, k), lambda j, i: (i, 0)),
                pl.BlockSpec((k, tn), lambda j, i: (0, jb + j))]
    args = [a, w]
    if head_g is not None:
        in_specs.append(pl.BlockSpec((1, LANES), lambda j, i: (0, 0)))
        args.append(head_g.reshape(1, LANES))
    if rope is not None:
        per = seq // tm
        for t in rope:
            in_specs.append(pl.BlockSpec((tm, LANES), lambda j, i: (i % per, 0)))
            args.append(t)
    return pl.pallas_call(
        functools.partial(_proj_kernel, head_norm=head_g is not None, rope=rope is not None),
        grid=(ncols // tn, n // tm),
        in_specs=in_specs,
        out_specs=pl.BlockSpec((tm, tn), lambda j, i: (i, j)),
        out_shape=jax.ShapeDtypeStruct((n, ncols), out_dtype),
        compiler_params=_params(("arbitrary", "arbitrary")),
        name="proj",
    )(*args)


def _attn_kernel(*refs, scale, has_cache):
    q_ref, k_ref, v_ref = refs[:3]
    o_ref = refs[-1]
    k = k_ref[...].astype(BF16)
    v = v_ref[...].astype(BF16)
    if has_cache:
        ck = refs[3][...].astype(BF16)
        cv = refs[4][...].astype(BF16)
    for g in range(KV_GROUP):
        cols = slice(g * LANES, (g + 1) * LANES)
        qh = q_ref[:, cols]
        s = _dot_nt(qh, k) * scale
        m = jnp.max(s, axis=-1, keepdims=True)
        if has_cache:
            s2 = _dot_nt(qh, ck) * scale
            m = jnp.maximum(m, jnp.max(s2, axis=-1, keepdims=True))
        p = jnp.exp(s - m)
        l = jnp.sum(p, axis=-1, keepdims=True)
        if has_cache:
            p2 = jnp.exp(s2 - m)
            l = l + jnp.sum(p2, axis=-1, keepdims=True)
        inv = 1.0 / l
        o = _dot((p * inv).astype(BF16), v)
        if has_cache:
            o = o + _dot((p2 * inv).astype(BF16), cv)
        o_ref[:, cols] = o.astype(o_ref.dtype)


def _attention(q, k, v, batch, seq, cache=None):
    n, qw = q.shape
    kvh = k.shape[1] // LANES
    gw = KV_GROUP * LANES
    tq = _tile(seq, 256)
    per = seq // tq
    in_specs = [pl.BlockSpec((tq, gw), lambda b, h, i: (b * per + i, h)),
                pl.BlockSpec((seq, LANES), lambda b, h, i: (b, h)),
                pl.BlockSpec((seq, LANES), lambda b, h, i: (b, h))]
    args = [q, k, v]
    if cache is not None:
        past = cache[0].shape[1]
        for c in cache:
            in_specs.append(pl.BlockSpec((None, past, LANES), lambda b, h, i: (b, 0, h)))
            args.append(c)
    return pl.pallas_call(
        functools.partial(_attn_kernel, scale=LANES ** -0.5, has_cache=cache is not None),
        grid=(batch, kvh, per),
        in_specs=in_specs,
        out_specs=pl.BlockSpec((tq, gw), lambda b, h, i: (b * per + i, h)),
        out_shape=jax.ShapeDtypeStruct((n, qw), BF16),
        compiler_params=_params(("arbitrary", "arbitrary", "arbitrary")),
        name="attention",
    )(*args)


def _pool_kernel(x_ref, w_ref, s_ref, o_ref):
    g = pl.program_id(1)
    x = x_ref[...]
    t = x.shape[0]
    row = lax.broadcasted_iota(jnp.int32, x.shape, 0)
    for gi, win in enumerate(POOL_WINDOWS):
        @pl.when(g == gi)
        def _():
            half = win // 2
            acc = x
            for d in range(-half, half):
                if d == 0:
                    continue
                shifted = pltpu.roll(x, (-d) % t, axis=0)
                valid = (row >= -d) if d < 0 else (row < t - d)
                acc = acc + jnp.where(valid, shifted, 0.0)
            cnt = (jnp.minimum(row + half, t) - jnp.maximum(row - half, 0)).astype(F32)
            pooled = acc / cnt - x
            mixed = _dot(pooled.astype(BF16), w_ref[...].astype(BF16)) * s_ref[...]
            o_ref[...] = mixed.astype(o_ref.dtype)


def _pool(xp, w_pool, pool_scale, batch, seq):
    n, pw = xp.shape
    ng, gd, _ = w_pool.shape
    return pl.pallas_call(
        _pool_kernel,
        grid=(batch, ng),
        in_specs=[pl.BlockSpec((seq, gd), lambda b, g: (b, g)),
                  pl.BlockSpec((None, gd, gd), lambda b, g: (g, 0, 0)),
                  pl.BlockSpec((1, gd), lambda b, g: (0, g))],
        out_specs=pl.BlockSpec((seq, gd), lambda b, g: (b, g)),
        out_shape=jax.ShapeDtypeStruct((n, pw), BF16),
        compiler_params=_params(("arbitrary", "arbitrary")),
        name="pool",
    )(xp, w_pool, pool_scale.reshape(1, pw))


def _outproj_kernel(a_ref, p_ref, wt_ref, wb_ref, o_ref):
    o_ref[...] = (_dot(a_ref[...], wt_ref[...].astype(BF16))
                  + _dot(p_ref[...], wb_ref[...].astype(BF16)))


def _outproj(attn, pool, w_out):
    n, aw = attn.shape
    pw = pool.shape[1]
    assert aw == pw
    d = w_out.shape[1]
    tm = _tile(n, 512)
    tn = _tile(d, 512)
    return pl.pallas_call(
        _outproj_kernel,
        grid=(d // tn, n // tm),
        in_specs=[pl.BlockSpec((tm, aw), lambda j, i: (i, 0)),
                  pl.BlockSpec((tm, pw), lambda j, i: (i, 0)),
                  pl.BlockSpec((aw, tn), lambda j, i: (0, j)),
                  pl.BlockSpec((pw, tn), lambda j, i: (1, j))],
        out_specs=pl.BlockSpec((tm, tn), lambda j, i: (i, j)),
        out_shape=jax.ShapeDtypeStruct((n, d), F32),
        compiler_params=_params(("arbitrary", "arbitrary")),
        name="outproj",
    )(attn, pool, w_out, w_out)


def _postmix_kernel(x_ref, mix_ref, gpost_ref, gpre_ref, m_ref, x1_ref, h2_ref):
    x1 = x_ref[...] + m_ref[0, 2:3, :] * _rms(mix_ref[...], gpost_ref[...])
    x1_ref[...] = x1
    y = _rms(x1, gpre_ref[...])
    h2_ref[...] = (y * (1.0 + m_ref[0, 4:5, :]) + m_ref[0, 3:4, :]).astype(h2_ref.dtype)


def _postmix(x, mix, g_post, g_pre, mods, seq):
    n, d = x.shape
    tm = _tile(seq, 256)
    row = pl.BlockSpec((tm, d), lambda i: (i, 0))
    vec = pl.BlockSpec((1, d), lambda i: (0, 0))
    return pl.pallas_call(
        _postmix_kernel,
        grid=(n // tm,),
        in_specs=[row, row, vec, vec, pl.BlockSpec((1, 6, d), _mods_map(mods, seq // tm))],
        out_specs=[row, row],
        out_shape=[jax.ShapeDtypeStruct((n, d), F32), jax.ShapeDtypeStruct((n, d), BF16)],
        compiler_params=_params(("arbitrary",)),
        name="postmix",
    )(x, mix, g_post.reshape(1, d), g_pre.reshape(1, d), mods)


def _top16(s, ids):
    width = s.shape[1]
    slot = lax.broadcasted_iota(jnp.int32, (PEER_TOPK, width), 0)
    vals = jnp.zeros((PEER_TOPK, width), F32)
    idxs = jnp.zeros((PEER_TOPK, width), F32)
    for k in range(PEER_TOPK):
        m = jnp.max(s, axis=0, keepdims=True)
        idx = jnp.min(jnp.where(s == m, ids, jnp.inf), axis=0, keepdims=True)
        vals = jnp.where(slot == k, m, vals)
        idxs = jnp.where(slot == k, idx, idxs)
        s = jnp.where(ids == idx, -jnp.inf, s)
    return vals, idxs.astype(jnp.int32)


def _row_ids(nrows, width):
    return lax.broadcasted_iota(jnp.int32, (nrows, width), 0).astype(F32)


def _pair_candidates(v1, v2):
    k = PEER_TOPK
    width = v1.shape[1]
    sub = 8
    a8 = lax.broadcasted_iota(jnp.int32, (sub, width), 0)
    vals = [v1 + v2[0:1, :]]
    pos = [_row_ids(k, width) * k]
    for b in range(1, sub):
        na = k // (b + 1)
        vals.append(jnp.where(a8 < na, v1[0:sub, :] + v2[b:b + 1, :], -jnp.inf))
        pos.append((a8 * k + b).astype(F32))
    vals.append(v1[0:1, :] + v2[sub:k, :])
    pos.append((a8 + sub).astype(F32))
    return jnp.concatenate(vals, axis=0), jnp.concatenate(pos, axis=0)


def _topk_kernel(q_ref, sk_ref, i_ref, j_ref, g_ref):
    half = sk_ref.shape[-1]
    nkeys = sk_ref.shape[-2]
    tt = q_ref.shape[0]
    ids = _row_ids(nkeys, tt)
    v1, i1 = _top16(_dot_nt(sk_ref[0].astype(BF16), q_ref[:, :half]), ids)
    v2, i2 = _top16(_dot_nt(sk_ref[1].astype(BF16), q_ref[:, half:]), ids)
    tv, pos = _top16(*_pair_candidates(v1, v2))
    a_sel = pos // PEER_TOPK
    b_sel = pos % PEER_TOPK
    isel = jnp.zeros_like(pos)
    jsel = jnp.zeros_like(pos)
    for a in range(PEER_TOPK):
        isel = jnp.where(a_sel == a, i1[a:a + 1, :], isel)
        jsel = jnp.where(b_sel == a, i2[a:a + 1, :], jsel)
    e = jnp.exp(tv - tv[0:1, :])
    i_ref[...] = isel
    j_ref[...] = jsel
    g_ref[...] = e / jnp.sum(e, axis=0, keepdims=True)


def _peer_topk(q, sub_keys):
    n = q.shape[0]
    heads, _, nkeys, half = sub_keys.shape
    tt = _tile(n, 256)
    out = pl.BlockSpec((None, PEER_TOPK, tt), lambda i, h: (h, 0, i))
    return pl.pallas_call(
        _topk_kernel,
        grid=(n // tt, heads),
        in_specs=[pl.BlockSpec((tt, 2 * half), lambda i, h: (i, h)),
                  pl.BlockSpec((None, 2, nkeys, half), lambda i, h: (h, 0, 0, 0))],
        out_specs=[out, out, out],
        out_shape=[jax.ShapeDtypeStruct((heads, PEER_TOPK, n), jnp.int32),
                   jax.ShapeDtypeStruct((heads, PEER_TOPK, n), jnp.int32),
                   jax.ShapeDtypeStruct((heads, PEER_TOPK, n), F32)],
        compiler_params=_params(("arbitrary", "arbitrary")),
        name="peer_topk",
    )(q, sub_keys)


def _gate_matrix_kernel(i_ref, j_ref, g_ref, o_ref, t_ref, *, nkeys):
    tile = i_ref.shape[0]
    sub = lax.broadcasted_iota(jnp.int32, (nkeys, i_ref.shape[1]), 0)

    def group(gidx, carry):
        base = pl.multiple_of(gidx * BF16_ROWS, BF16_ROWS)
        for r in range(BF16_ROWS):
            irow = i_ref[pl.ds(base + r, 1), :]
            jrow = j_ref[pl.ds(base + r, 1), :]
            grow = g_ref[pl.ds(base + r, 1), :]
            a = jnp.where(sub == irow, grow, 0.0).astype(BF16)
            b = jnp.where(sub == jrow, 1.0, 0.0).astype(BF16)
            t_ref[r * GM_PITCH:r * GM_PITCH + nkeys, :] = _dot_nt(a, b)
        for i in range(nkeys):
            rows = t_ref[pl.ds(i, BF16_ROWS, stride=GM_PITCH), :]
            o_ref[pl.ds(base, BF16_ROWS), i * nkeys:(i + 1) * nkeys] = rows.astype(o_ref.dtype)
        return carry

    lax.fori_loop(0, tile // BF16_ROWS, group, 0)


def _gate_matrix(isel, jsel, gate, nkeys):
    n, s = isel.shape
    assert nkeys == LANES
    tile = _tile(n, 128)
    row = pl.BlockSpec((tile, s), lambda i: (i, 0))
    return pl.pallas_call(
        functools.partial(_gate_matrix_kernel, nkeys=nkeys),
        grid=(n // tile,),
        in_specs=[row, row, row],
        out_specs=pl.BlockSpec((tile, nkeys * nkeys), lambda i: (i, 0)),
        out_shape=jax.ShapeDtypeStruct((n, nkeys * nkeys), BF16),
        scratch_shapes=[pltpu.VMEM((BF16_ROWS * GM_PITCH, nkeys), F32)],
        compiler_params=_params(("arbitrary",)),
        name="gate_matrix",
    )(isel, jsel, gate)


def _gelu_tanh(x):
    return x * (0.5 * (1.0 + jnp.tanh(math.sqrt(2.0 / math.pi) * (x + 0.044715 * (x * x * x)))))


def _expert_kernel(h_ref, gm_ref, u_ref, v_ref, o_ref):
    @pl.when(pl.program_id(1) == 0)
    def _():
        o_ref[...] = jnp.zeros_like(o_ref)

    h = h_ref[...]
    te = u_ref.shape[0]
    sub = min(te, EXPERT_SUBTILE)
    ws = []
    for r in range(0, te, sub):
        act = _gelu_tanh(_dot_nt(h, u_ref[r:r + sub, :]))
        ws.append((gm_ref[:, r:r + sub].astype(F32) * act).astype(BF16))
    w = jnp.concatenate(ws, axis=1)
    chunk = min(o_ref.shape[1], EXPERT_OUT_CHUNK)
    for c in range(0, o_ref.shape[1], chunk):
        o_ref[:, c:c + chunk] += _dot(w, v_ref[:, c:c + chunk])


def _experts(h, gm, u, v):
    n, d = h.shape
    ne = u.shape[0]
    tb = _tile(n, 512)
    te = _tile(ne, 512)
    return pl.pallas_call(
        _expert_kernel,
        grid=(n // tb, ne // te),
        in_specs=[pl.BlockSpec((tb, d), lambda i, e: (i, 0)),
                  pl.BlockSpec((tb, te), lambda i, e: (i, e)),
                  pl.BlockSpec((te, d), lambda i, e: (e, 0)),
                  pl.BlockSpec((te, d), lambda i, e: (e, 0))],
        out_specs=pl.BlockSpec((tb, d), lambda i, e: (i, 0)),
        out_shape=jax.ShapeDtypeStruct((n, d), F32),
        compiler_params=_params(("arbitrary", "arbitrary")),
        name="experts",
    )(h, gm, u, v)


def _final_kernel(x_ref, f_ref, g_ref, m_ref, o_ref):
    o_ref[...] = x_ref[...] + m_ref[0, 5:6, :] * _rms(f_ref[...], g_ref[...])


def _final(x1, f, g_post, mods, seq):
    n, d = x1.shape
    tm = _tile(seq, 256)
    row = pl.BlockSpec((tm, d), lambda i: (i, 0))
    return pl.pallas_call(
        _final_kernel,
        grid=(n // tm,),
        in_specs=[row, row, pl.BlockSpec((1, d), lambda i: (0, 0)),
                  pl.BlockSpec((1, 6, d), _mods_map(mods, seq // tm))],
        out_specs=row,
        out_shape=jax.ShapeDtypeStruct((n, d), F32),
        compiler_params=_params(("arbitrary",)),
        name="final",
    )(x1, f, g_post.reshape(1, d), mods)


def _rope_tables(seq):
    pairs = LANES // 4
    t = jnp.arange(seq)
    inv = ROPE_THETA ** (-jnp.arange(pairs, dtype=F32) / pairs)
    ang_r = (t // GRID_W).astype(F32)[:, None] * inv
    ang_c = (t % GRID_W).astype(F32)[:, None] * inv
    cos = jnp.concatenate([jnp.cos(ang_r)] * 2 + [jnp.cos(ang_c)] * 2, axis=-1)
    sin = jnp.concatenate([-jnp.sin(ang_r), jnp.sin(ang_r), -jnp.sin(ang_c), jnp.sin(ang_c)], axis=-1)
    return cos, sin


def _trunk(x, mods, cache, rope, lw):
    (pre_mix_g, post_mix_g, pre_ffn_g, post_ffn_g, w_in, q_norm_g, k_norm_g, w_pool, pool_scale,
     w_out, peer_w_query, peer_sub_keys, peer_u, peer_v) = lw
    batch, seq, d = x.shape
    n = batch * seq
    x2 = x.reshape(n, d)
    attn_w = d // 2
    kv_w = attn_w // KV_GROUP

    h = _prenorm(x2, pre_mix_g, mods, seq)
    q = _proj(h, w_in, 0, attn_w, BF16, head_g=q_norm_g, rope=rope, seq=seq)
    k = _proj(h, w_in, attn_w, kv_w, F32, head_g=k_norm_g, rope=rope, seq=seq)
    v = _proj(h, w_in, attn_w + kv_w, kv_w, F32)
    xp = _proj(h, w_in, attn_w + 2 * kv_w, d - attn_w, F32)
    attn = _attention(q, k, v, batch, seq, cache)
    pool = _pool(xp, w_pool, pool_scale, batch, seq)
    mix = _outproj(attn, pool, w_out)
    x1, h2 = _postmix(x2, mix, post_mix_g, pre_ffn_g, mods, seq)

    heads, _, nkeys, _ = peer_sub_keys.shape
    pq = _proj(h2, peer_w_query, 0, peer_w_query.shape[1], BF16)
    isel, jsel, gate = _peer_topk(pq, peer_sub_keys)
    flat = lambda a: a.reshape(heads * PEER_TOPK, n).T
    gm = _gate_matrix(flat(isel), flat(jsel), flat(gate), nkeys)
    f = _experts(h2, gm, peer_u, peer_v)
    y = _final(x1, f, post_ffn_g, mods, seq)
    return y.reshape(batch, seq, d), k, v


def kernel(x_prompt, x_sample, cache_k, cache_v, c, c_ctx, w_mod, b_mod, pre_mix_g, post_mix_g,
           pre_ffn_g, post_ffn_g, w_in, q_norm_g, k_norm_g, w_pool, pool_scale, w_out,
           peer_w_query, peer_sub_keys, peer_u, peer_v):
    depth = w_mod.shape[0]
    dec_batch, dec_seq, _ = x_sample.shape
    batch, seq, _ = x_prompt.shape
    kvh, hd = cache_k.shape[3], cache_k.shape[4]
    rope = _rope_tables(dec_seq)
    cond = jnp.concatenate([c_ctx[None, :], c], axis=0)
    y_prompt, y_sample = x_prompt, x_sample
    new_k, new_v = [], []
    for l in range(depth):
        mods = _mods(cond, w_mod[l], b_mod[l])
        lw = [p[l] for p in (pre_mix_g, post_mix_g, pre_ffn_g, post_ffn_g, w_in, q_norm_g, k_norm_g,
                             w_pool, pool_scale, w_out, peer_w_query, peer_sub_keys)]
        lw += [peer_u[l].astype(BF16), peer_v[l].astype(BF16)]
        y_prompt, k_l, v_l = _trunk(y_prompt, mods[:1], None, None, lw)
        new_k.append(k_l.reshape(batch, seq, kvh, hd))
        new_v.append(v_l.reshape(batch, seq, kvh, hd))
        cache = (cache_k[:, l].reshape(dec_batch, -1, kvh * hd),
                 cache_v[:, l].reshape(dec_batch, -1, kvh * hd))
        y_sample, _, _ = _trunk(y_sample, mods[1:], cache, rope, lw)
    return (y_prompt, y_sample, jnp.stack(new_k, axis=1), jnp.stack(new_v, axis=1))
```

```python
import functools
import math
from typing import NamedTuple

import jax
import jax.numpy as jnp
from jax import lax
from jax.experimental import pallas as pl
from jax.experimental.pallas import tpu as pltpu

F32 = jnp.float32
BF16 = jnp.bfloat16

EPS = 1e-6
GRID_W = 64
ROPE_THETA = 10000.0
KV_GROUP = 4
POOL_WINDOWS = (2, 4, 8, 16)
PEER_TOPK = 16
LANES = 128
SUBLANES = 8
BF16_ROWS = 16
GM_PITCH = 136
EXPERT_ROW_SPLIT = 512
EXPERT_OUT_CHUNK = 512
ATTN_SMALL_SEQ = 256
VMEM_LIMIT = 56 * 1024 * 1024


def _params(semantics):
    return pltpu.CompilerParams(dimension_semantics=semantics, vmem_limit_bytes=VMEM_LIMIT)


def _dot(a, b):
    return jnp.dot(a, b, preferred_element_type=F32)


def _dot_nt(a, b):
    return lax.dot_general(a, b, (((1,), (1,)), ((), ())), preferred_element_type=F32)


def _rms(x, g):
    return x * lax.rsqrt(jnp.mean(x * x, axis=-1, keepdims=True) + EPS) * g


def _tile(n, pref):
    return pref if n % pref == 0 else n


def _mods_map(mods, tiles_per_seq):
    if mods.shape[0] == 1:
        return lambda i: (0, 0, 0)
    return lambda i: (i // tiles_per_seq, 0, 0)


def _mods_kernel(c_ref, w_ref, b_ref, o_ref):
    c = c_ref[...]
    s = (c * jax.nn.sigmoid(c)).astype(BF16)
    o_ref[...] = _dot(s, w_ref[...].astype(BF16)) + b_ref[...]


def _mods(cond, w_mod, b_mod, ncols):
    rp, d = cond.shape
    tn = _tile(ncols, 512)
    return pl.pallas_call(
        _mods_kernel,
        grid=(ncols // tn,),
        in_specs=[pl.BlockSpec((rp, d), lambda j: (0, 0)),
                  pl.BlockSpec((d, tn), lambda j: (0, j)),
                  pl.BlockSpec((1, tn), lambda j: (0, j))],
        out_specs=pl.BlockSpec((rp, tn), lambda j: (0, j)),
        out_shape=jax.ShapeDtypeStruct((rp, ncols), F32),
        compiler_params=_params(("arbitrary",)),
        name="mods",
    )(cond, w_mod, b_mod)


def _prenorm_kernel(x_ref, g_ref, m_ref, o_ref):
    y = _rms(x_ref[...], g_ref[...])
    o_ref[...] = (y * (1.0 + m_ref[0, 1:2, :]) + m_ref[0, 0:1, :]).astype(o_ref.dtype)


def _prenorm(x, g, mods, seq):
    n, d = x.shape
    tm = _tile(seq, 256)
    return pl.pallas_call(
        _prenorm_kernel,
        grid=(n // tm,),
        in_specs=[pl.BlockSpec((tm, d), lambda i: (i, 0)),
                  pl.BlockSpec((1, d), lambda i: (0, 0)),
                  pl.BlockSpec((1,) + mods.shape[1:], _mods_map(mods, seq // tm))],
        out_specs=pl.BlockSpec((tm, d), lambda i: (i, 0)),
        out_shape=jax.ShapeDtypeStruct((n, d), BF16),
        compiler_params=_params(("arbitrary",)),
        name="prenorm",
    )(x, g.reshape(1, d), mods)


def _rope(y, cos, sin):
    lane = lax.broadcasted_iota(jnp.int32, y.shape, 1)
    quarter = LANES // 4
    swapped = jnp.where((lane & quarter) == 0,
                        pltpu.roll(y, LANES - quarter, axis=1),
                        pltpu.roll(y, quarter, axis=1))
    return y * cos + swapped * sin


class _CastSide(NamedTuple):
    table: jax.Array


class _ModsSide(NamedTuple):
    cond: jax.Array
    w_mod: jax.Array
    b_mod: jax.Array
    col0: int
    ncols: int


_SIDE_INPUTS = {None: 0, _CastSide: 1, _ModsSide: 3}


def _proj_kernel(*refs, n_ops, head_norm, rope, side):
    it = iter(refs)
    ops = [(next(it), next(it)) for _ in range(n_ops)]
    g_ref = next(it) if head_norm else None
    cos_ref, sin_ref = (next(it), next(it)) if rope else (None, None)
    side_in = [next(it) for _ in range(_SIDE_INPUTS[side])]
    o_ref = next(it)
    if side is _CastSide:
        side_out = next(it)
        side_out[...] = side_in[0][...].astype(side_out.dtype)
    elif side is _ModsSide:
        _mods_kernel(*side_in, next(it))

    acc = None
    for a_ref, w_ref in ops:
        part = _dot(a_ref[...], w_ref[...].astype(BF16))
        acc = part if acc is None else acc + part
    if not head_norm:
        o_ref[...] = acc.astype(o_ref.dtype)
        return
    g = g_ref[...]
    for h in range(acc.shape[1] // LANES):
        cols = slice(h * LANES, (h + 1) * LANES)
        y = _rms(acc[:, cols], g)
        if rope:
            y = _rope(y, cos_ref[...], sin_ref[...])
        o_ref[:, cols] = y.astype(o_ref.dtype)


def _proj(ops, w, col0, ncols, out_dtype, head_g=None, rope=None, seq=None, side=None):
    n = ops[0].shape[0]
    tm = _tile(n if rope is None else seq, 512)
    tn = math.gcd(math.gcd(ncols, col0), 512)
    assert tn % LANES == 0
    jb = col0 // tn
    nj, ni = ncols // tn, n // tm
    step = lambda j, i: j * ni + i
    in_specs, args = [], []
    row0 = 0
    for a in ops:
        k = a.shape[1]
        assert row0 % k == 0
        in_specs += [pl.BlockSpec((tm, k), lambda j, i: (i, 0)),
                     pl.BlockSpec((k, tn), lambda j, i, rb=row0 // k: (rb, jb + j))]
        args += [a, w]
        row0 += k
    if head_g is not None:
        in_specs.append(pl.BlockSpec((1, LANES), lambda j, i: (0, 0)))
        args.append(head_g.reshape(1, LANES))
    if rope is not None:
        per = seq // tm
        for t in rope:
            in_specs.append(pl.BlockSpec((tm, LANES), lambda j, i: (i % per, 0)))
            args.append(t)
    out_specs = [pl.BlockSpec((tm, tn), lambda j, i: (i, j))]
    out_shape = [jax.ShapeDtypeStruct((n, ncols), out_dtype)]
    if isinstance(side, _CastSide):
        rows, width = side.table.shape
        rs = rows // (nj * ni)
        assert rs * nj * ni == rows and rs % BF16_ROWS == 0
        blk = pl.BlockSpec((rs, width), lambda j, i: (step(j, i), 0))
        in_specs.append(blk)
        args.append(side.table)
        out_specs.append(blk)
        out_shape.append(jax.ShapeDtypeStruct((rows, width), BF16))
    elif isinstance(side, _ModsSide):
        rp, d = side.cond.shape
        bw = side.ncols // (nj * ni)
        assert bw * nj * ni == side.ncols and bw % LANES == 0 and side.col0 % bw == 0
        cb = side.col0 // bw
        in_specs += [pl.BlockSpec((rp, d), lambda j, i: (0, 0)),
                     pl.BlockSpec((d, bw), lambda j, i: (0, cb + step(j, i))),
                     pl.BlockSpec((1, bw), lambda j, i: (0, cb + step(j, i)))]
        args += [side.cond, side.w_mod, side.b_mod]
        out_specs.append(pl.BlockSpec((rp, bw), lambda j, i: (0, step(j, i))))
        out_shape.append(jax.ShapeDtypeStruct((rp, side.ncols), F32))
    outs = pl.pallas_call(
        functools.partial(_proj_kernel, n_ops=len(ops), head_norm=head_g is not None,
                          rope=rope is not None, side=None if side is None else type(side)),
        grid=(nj, ni),
        in_specs=in_specs,
        out_specs=out_specs,
        out_shape=out_shape,
        compiler_params=_params(("arbitrary", "arbitrary")),
        name="proj",
    )(*args)
    return outs[0] if side is None else tuple(outs)


def _attn_kernel(*refs, scale, has_cache):
    q_ref, k_ref, v_ref = refs[:3]
    o_ref = refs[-1]
    for kv in range(k_ref.shape[1] // LANES):
        kcols = slice(kv * LANES, (kv + 1) * LANES)
        k = k_ref[:, kcols].astype(BF16)
        v = v_ref[:, kcols].astype(BF16)
        if has_cache:
            ck = refs[3][:, kcols].astype(BF16)
            cv = refs[4][:, kcols].astype(BF16)
        for g in range(KV_GROUP):
            cols = slice((kv * KV_GROUP + g) * LANES, (kv * KV_GROUP + g + 1) * LANES)
            qh = q_ref[:, cols]
            s = _dot_nt(qh, k) * scale
            m = jnp.max(s, axis=-1, keepdims=True)
            if has_cache:
                s2 = _dot_nt(qh, ck) * scale
                m = jnp.maximum(m, jnp.max(s2, axis=-1, keepdims=True))
            p = jnp.exp(s - m)
            l = jnp.sum(p, axis=-1, keepdims=True)
            if has_cache:
                p2 = jnp.exp(s2 - m)
                l = l + jnp.sum(p2, axis=-1, keepdims=True)
            inv = 1.0 / l
            o = _dot((p * inv).astype(BF16), v)
            if has_cache:
                o = o + _dot((p2 * inv).astype(BF16), cv)
            o_ref[:, cols] = o.astype(o_ref.dtype)


def _attention(q, k, v, batch, seq, cache=None):
    n, qw = q.shape
    kvh = k.shape[1] // LANES
    hps = kvh if seq <= ATTN_SMALL_SEQ else 1
    kw = hps * LANES
    gw = KV_GROUP * kw
    tq = _tile(seq, 256)
    per = seq // tq
    in_specs = [pl.BlockSpec((tq, gw), lambda b, h, i: (b * per + i, h)),
                pl.BlockSpec((seq, kw), lambda b, h, i: (b, h)),
                pl.BlockSpec((seq, kw), lambda b, h, i: (b, h))]
    args = [q, k, v]
    if cache is not None:
        past = cache[0].shape[1]
        for c in cache:
            in_specs.append(pl.BlockSpec((None, past, kw), lambda b, h, i: (b, 0, h)))
            args.append(c)
    return pl.pallas_call(
        functools.partial(_attn_kernel, scale=LANES ** -0.5, has_cache=cache is not None),
        grid=(batch, kvh // hps, per),
        in_specs=in_specs,
        out_specs=pl.BlockSpec((tq, gw), lambda b, h, i: (b * per + i, h)),
        out_shape=jax.ShapeDtypeStruct((n, qw), BF16),
        compiler_params=_params(("arbitrary", "arbitrary", "arbitrary")),
        name="attention",
    )(*args)


def _pool_kernel(x_ref, w_ref, s_ref, o_ref):
    g = pl.program_id(1)
    x = x_ref[...]
    t = x.shape[0]
    row = lax.broadcasted_iota(jnp.int32, x.shape, 0)
    for gi, win in enumerate(POOL_WINDOWS):
        @pl.when(g == gi)
        def _():
            half = win // 2
            acc = x
            for d in range(-half, half):
                if d == 0:
                    continue
                shifted = pltpu.roll(x, (-d) % t, axis=0)
                valid = (row >= -d) if d < 0 else (row < t - d)
                acc = acc + jnp.where(valid, shifted, 0.0)
            cnt = (jnp.minimum(row + half, t) - jnp.maximum(row - half, 0)).astype(F32)
            pooled = acc / cnt - x
            mixed = _dot(pooled.astype(BF16), w_ref[...].astype(BF16)) * s_ref[...]
            o_ref[...] = mixed.astype(o_ref.dtype)


def _pool(xp, w_pool, pool_scale, batch, seq):
    n, pw = xp.shape
    ng, gd, _ = w_pool.shape
    return pl.pallas_call(
        _pool_kernel,
        grid=(batch, ng),
        in_specs=[pl.BlockSpec((seq, gd), lambda b, g: (b, g)),
                  pl.BlockSpec((None, gd, gd), lambda b, g: (g, 0, 0)),
                  pl.BlockSpec((1, gd), lambda b, g: (0, g))],
        out_specs=pl.BlockSpec((seq, gd), lambda b, g: (b, g)),
        out_shape=jax.ShapeDtypeStruct((n, pw), BF16),
        compiler_params=_params(("arbitrary", "arbitrary")),
        name="pool",
    )(xp, w_pool, pool_scale.reshape(1, pw))


def _postmix_kernel(x_ref, mix_ref, gpost_ref, gpre_ref, m_ref, x1_ref, h2_ref):
    x1 = x_ref[...] + m_ref[0, 2:3, :] * _rms(mix_ref[...], gpost_ref[...])
    x1_ref[...] = x1
    y = _rms(x1, gpre_ref[...])
    h2_ref[...] = (y * (1.0 + m_ref[0, 4:5, :]) + m_ref[0, 3:4, :]).astype(h2_ref.dtype)


def _postmix(x, mix, g_post, g_pre, mods, seq):
    n, d = x.shape
    tm = _tile(seq, 256)
    row = pl.BlockSpec((tm, d), lambda i: (i, 0))
    vec = pl.BlockSpec((1, d), lambda i: (0, 0))
    return pl.pallas_call(
        _postmix_kernel,
        grid=(n // tm,),
        in_specs=[row, row, vec, vec, pl.BlockSpec((1, 6, d), _mods_map(mods, seq // tm))],
        out_specs=[row, row],
        out_shape=[jax.ShapeDtypeStruct((n, d), F32), jax.ShapeDtypeStruct((n, d), BF16)],
        compiler_params=_params(("arbitrary",)),
        name="postmix",
    )(x, mix, g_post.reshape(1, d), g_pre.reshape(1, d), mods)


def _final_kernel(x_ref, f_ref, g_ref, m_ref, o_ref):
    o_ref[...] = x_ref[...] + m_ref[0, 5:6, :] * _rms(f_ref[...], g_ref[...])


def _final(x1, f, g_post, mods, seq):
    n, d = x1.shape
    tm = _tile(seq, 256)
    row = pl.BlockSpec((tm, d), lambda i: (i, 0))
    return pl.pallas_call(
        _final_kernel,
        grid=(n // tm,),
        in_specs=[row, row, pl.BlockSpec((1, d), lambda i: (0, 0)),
                  pl.BlockSpec((1, 6, d), _mods_map(mods, seq // tm))],
        out_specs=row,
        out_shape=jax.ShapeDtypeStruct((n, d), F32),
        compiler_params=_params(("arbitrary",)),
        name="final",
    )(x1, f, g_post.reshape(1, d), mods)


def _top16(s, ids):
    width = s.shape[1]
    slot = lax.broadcasted_iota(jnp.int32, (PEER_TOPK, width), 0)
    vals = jnp.zeros((PEER_TOPK, width), F32)
    idxs = jnp.zeros((PEER_TOPK, width), F32)
    for k in range(PEER_TOPK):
        m = jnp.max(s, axis=0, keepdims=True)
        idx = jnp.min(jnp.where(s == m, ids, jnp.inf), axis=0, keepdims=True)
        vals = jnp.where(slot == k, m, vals)
        idxs = jnp.where(slot == k, idx, idxs)
        s = jnp.where(ids == idx, -jnp.inf, s)
    return vals, idxs.astype(jnp.int32)


def _row_ids(nrows, width):
    return lax.broadcasted_iota(jnp.int32, (nrows, width), 0).astype(F32)


def _pair_candidates(v1, v2):
    k = PEER_TOPK
    width = v1.shape[1]
    a8 = lax.broadcasted_iota(jnp.int32, (SUBLANES, width), 0)
    vals = [v1 + v2[0:1, :]]
    pos = [_row_ids(k, width) * k]
    for b in range(1, SUBLANES):
        na = k // (b + 1)
        vals.append(jnp.where(a8 < na, v1[0:SUBLANES, :] + v2[b:b + 1, :], -jnp.inf))
        pos.append((a8 * k + b).astype(F32))
    vals.append(v1[0:1, :] + v2[SUBLANES:k, :])
    pos.append((a8 + SUBLANES).astype(F32))
    return jnp.concatenate(vals, axis=0), jnp.concatenate(pos, axis=0)


def _topk_kernel(q_ref, sk_ref, i_ref, j_ref, g_ref):
    half = sk_ref.shape[-1]
    nkeys = sk_ref.shape[-2]
    tt = q_ref.shape[0]
    ids = _row_ids(nkeys, tt)
    v1, i1 = _top16(_dot_nt(sk_ref[0].astype(BF16), q_ref[:, :half]), ids)
    v2, i2 = _top16(_dot_nt(sk_ref[1].astype(BF16), q_ref[:, half:]), ids)
    tv, pos = _top16(*_pair_candidates(v1, v2))
    a_sel = pos // PEER_TOPK
    b_sel = pos % PEER_TOPK
    isel = jnp.zeros_like(pos)
    jsel = jnp.zeros_like(pos)
    for a in range(PEER_TOPK):
        isel = jnp.where(a_sel == a, i1[a:a + 1, :], isel)
        jsel = jnp.where(b_sel == a, i2[a:a + 1, :], jsel)
    e = jnp.exp(tv - tv[0:1, :])
    i_ref[...] = isel
    j_ref[...] = jsel
    g_ref[...] = e / jnp.sum(e, axis=0, keepdims=True)


def _peer_topk(q, sub_keys):
    n = q.shape[0]
    heads, _, nkeys, half = sub_keys.shape
    tt = _tile(n, 512)
    out = pl.BlockSpec((None, PEER_TOPK, tt), lambda i, h: (h, 0, i))
    return pl.pallas_call(
        _topk_kernel,
        grid=(n // tt, heads),
        in_specs=[pl.BlockSpec((tt, 2 * half), lambda i, h: (i, h)),
                  pl.BlockSpec((None, 2, nkeys, half), lambda i, h: (h, 0, 0, 0))],
        out_specs=[out, out, out],
        out_shape=[jax.ShapeDtypeStruct((heads, PEER_TOPK, n), jnp.int32),
                   jax.ShapeDtypeStruct((heads, PEER_TOPK, n), jnp.int32),
                   jax.ShapeDtypeStruct((heads, PEER_TOPK, n), F32)],
        compiler_params=_params(("arbitrary", "arbitrary")),
        name="peer_topk",
    )(q, sub_keys)


def _gate_matrix_kernel(i_ref, j_ref, g_ref, o_ref, t_ref, *, nkeys):
    tile = i_ref.shape[0]
    sub = lax.broadcasted_iota(jnp.int32, (nkeys, i_ref.shape[1]), 0)

    def group(gidx, carry):
        base = pl.multiple_of(gidx * BF16_ROWS, BF16_ROWS)
        for r in range(BF16_ROWS):
            irow = i_ref[pl.ds(base + r, 1), :]
            jrow = j_ref[pl.ds(base + r, 1), :]
            grow = g_ref[pl.ds(base + r, 1), :]
            a = jnp.where(sub == irow, grow, 0.0).astype(BF16)
            b = jnp.where(sub == jrow, 1.0, 0.0).astype(BF16)
            t_ref[r * GM_PITCH:r * GM_PITCH + nkeys, :] = _dot_nt(a, b)
        for i in range(nkeys):
            rows = t_ref[pl.ds(i, BF16_ROWS, stride=GM_PITCH), :]
            o_ref[pl.ds(base, BF16_ROWS), i * nkeys:(i + 1) * nkeys] = rows.astype(o_ref.dtype)
        return carry

    lax.fori_loop(0, tile // BF16_ROWS, group, 0)


def _gate_matrix(isel, jsel, gate, nkeys):
    n, s = isel.shape
    assert nkeys == LANES
    tile = _tile(n, 128)
    row = pl.BlockSpec((tile, s), lambda i: (i, 0))
    return pl.pallas_call(
        functools.partial(_gate_matrix_kernel, nkeys=nkeys),
        grid=(n // tile,),
        in_specs=[row, row, row],
        out_specs=pl.BlockSpec((tile, nkeys * nkeys), lambda i: (i, 0)),
        out_shape=jax.ShapeDtypeStruct((n, nkeys * nkeys), BF16),
        scratch_shapes=[pltpu.VMEM((BF16_ROWS * GM_PITCH, nkeys), F32)],
        compiler_params=_params(("arbitrary",)),
        name="gate_matrix",
    )(isel, jsel, gate)


def _gelu_tanh(x):
    return x * (0.5 * (1.0 + jnp.tanh(math.sqrt(2.0 / math.pi) * (x + 0.044715 * (x * x * x)))))


def _expert_kernel(h_ref, gm_ref, u_ref, v_ref, o_ref):
    @pl.when(pl.program_id(1) == 0)
    def _():
        o_ref[...] = jnp.zeros_like(o_ref)

    tb, d = o_ref.shape
    half = min(tb, EXPERT_ROW_SPLIT)
    chunk = min(d, EXPERT_OUT_CHUNK)
    for r in range(0, tb, half):
        rows = slice(r, r + half)
        act = _gelu_tanh(_dot_nt(h_ref[rows, :], u_ref[...]))
        w = (gm_ref[rows, :].astype(F32) * act).astype(BF16)
        for c in range(0, d, chunk):
            o_ref[rows, c:c + chunk] += _dot(w, v_ref[:, c:c + chunk])


def _experts(h, gm, u, v):
    n, d = h.shape
    ne = u.shape[0]
    tb = _tile(n, 1024)
    te = _tile(ne, 512)
    once = pl.Buffered(1)
    return pl.pallas_call(
        _expert_kernel,
        grid=(n // tb, ne // te),
        in_specs=[pl.BlockSpec((tb, d), lambda i, e: (i, 0), pipeline_mode=once),
                  pl.BlockSpec((tb, te), lambda i, e: (i, e)),
                  pl.BlockSpec((te, d), lambda i, e: (e, 0)),
                  pl.BlockSpec((te, d), lambda i, e: (e, 0))],
        out_specs=pl.BlockSpec((tb, d), lambda i, e: (i, 0), pipeline_mode=once),
        out_shape=jax.ShapeDtypeStruct((n, d), F32),
        compiler_params=_params(("arbitrary", "arbitrary")),
        name="experts",
    )(h, gm, u, v)


def _rope_tables(seq):
    pairs = LANES // 4
    t = jnp.arange(seq)
    inv = ROPE_THETA ** (-jnp.arange(pairs, dtype=F32) / pairs)
    ang_r = (t // GRID_W).astype(F32)[:, None] * inv
    ang_c = (t % GRID_W).astype(F32)[:, None] * inv
    cos = jnp.concatenate([jnp.cos(ang_r)] * 2 + [jnp.cos(ang_c)] * 2, axis=-1)
    sin = jnp.concatenate([-jnp.sin(ang_r), jnp.sin(ang_r), -jnp.sin(ang_c), jnp.sin(ang_c)], axis=-1)
    return cos, sin


class _LayerWeights(NamedTuple):
    pre_mix_g: jax.Array
    post_mix_g: jax.Array
    pre_ffn_g: jax.Array
    post_ffn_g: jax.Array
    w_in: jax.Array
    q_norm_g: jax.Array
    k_norm_g: jax.Array
    w_pool: jax.Array
    pool_scale: jax.Array
    w_out: jax.Array
    peer_w_query: jax.Array
    peer_sub_keys: jax.Array


class _Shared(NamedTuple):
    mods: jax.Array
    peer_u: jax.Array
    peer_v: jax.Array


class _Host(NamedTuple):
    cond: jax.Array
    w_mod: jax.Array
    b_mod: jax.Array
    mods_head: jax.Array
    peer_u: jax.Array
    peer_v: jax.Array


def _trunk(x, mod_rows, mods_head, cache, rope, lw, shared):
    batch, seq, d = x.shape
    n = batch * seq
    x2 = x.reshape(n, d)
    attn_w = d // 2
    kv_w = attn_w // KV_GROUP
    host = shared if isinstance(shared, _Host) else None
    mods_side = lambda part: None if host is None else _ModsSide(host.cond, host.w_mod, host.b_mod,
                                                                  (1 + part) * 2 * d, 2 * d)

    h = _prenorm(x2, lw.pre_mix_g, mods_head[mod_rows], seq)
    q = _proj([h], lw.w_in, 0, attn_w, BF16, head_g=lw.q_norm_g, rope=rope, seq=seq, side=mods_side(0))
    k = _proj([h], lw.w_in, attn_w, kv_w, F32, head_g=lw.k_norm_g, rope=rope, seq=seq)
    v = _proj([h], lw.w_in, attn_w + kv_w, kv_w, F32)
    xp = _proj([h], lw.w_in, attn_w + 2 * kv_w, d - attn_w, F32, side=mods_side(1))
    if host is not None:
        (q, mods_mid), (xp, mods_tail) = q, xp
        mods = jnp.concatenate([host.mods_head.reshape(-1, 2 * d), mods_mid, mods_tail], axis=1)
        mods = mods.reshape(-1, 6, d)
    else:
        mods = shared.mods
    attn = _attention(q, k, v, batch, seq, cache)
    pool = _pool(xp, lw.w_pool, lw.pool_scale, batch, seq)
    mix = _proj([attn, pool], lw.w_out, 0, d, F32, side=None if host is None else _CastSide(host.peer_u))
    if host is not None:
        mix, peer_u = mix
    else:
        peer_u = shared.peer_u
    x1, h2 = _postmix(x2, mix, lw.post_mix_g, lw.pre_ffn_g, mods[mod_rows], seq)

    heads, _, nkeys, _ = lw.peer_sub_keys.shape
    pq = _proj([h2], lw.peer_w_query, 0, lw.peer_w_query.shape[1], BF16,
               side=None if host is None else _CastSide(host.peer_v))
    if host is not None:
        pq, peer_v = pq
    else:
        peer_v = shared.peer_v
    isel, jsel, gate = _peer_topk(pq, lw.peer_sub_keys)
    flat = lambda a: a.reshape(heads * PEER_TOPK, n).T
    gm = _gate_matrix(flat(isel), flat(jsel), flat(gate), nkeys)
    f = _experts(h2, gm, peer_u, peer_v)
    y = _final(x1, f, lw.post_ffn_g, mods[mod_rows], seq)
    return y.reshape(batch, seq, d), k, v, _Shared(mods, peer_u, peer_v)


def kernel(x_prompt, x_sample, cache_k, cache_v, c, c_ctx, w_mod, b_mod, pre_mix_g, post_mix_g,
           pre_ffn_g, post_ffn_g, w_in, q_norm_g, k_norm_g, w_pool, pool_scale, w_out,
           peer_w_query, peer_sub_keys, peer_u, peer_v):
    depth, d, _ = w_mod.shape
    dec_batch, dec_seq, _ = x_sample.shape
    batch, seq, _ = x_prompt.shape
    kvh, hd = cache_k.shape[3], cache_k.shape[4]
    rope = _rope_tables(dec_seq)
    cond = jnp.concatenate([c_ctx[None, :], c], axis=0)
    cond = jnp.zeros((SUBLANES, d), F32).at[:cond.shape[0]].set(cond)
    prompt_rows, sample_rows = slice(0, 1), slice(1, 1 + dec_batch)
    y_prompt, y_sample = x_prompt, x_sample
    new_k, new_v = [], []
    for l in range(depth):
        lw = _LayerWeights(*(p[l] for p in (pre_mix_g, post_mix_g, pre_ffn_g, post_ffn_g, w_in, q_norm_g,
                                            k_norm_g, w_pool, pool_scale, w_out, peer_w_query,
                                            peer_sub_keys)))
        b_l = b_mod[l].reshape(1, -1)
        mods_head = _mods(cond, w_mod[l], b_l, 2 * d).reshape(SUBLANES, 2, d)
        host = _Host(cond, w_mod[l], b_l, mods_head, peer_u[l], peer_v[l])
        y_prompt, k_l, v_l, shared = _trunk(y_prompt, prompt_rows, mods_head, None, None, lw, host)
        new_k.append(k_l.reshape(batch, seq, kvh, hd))
        new_v.append(v_l.reshape(batch, seq, kvh, hd))
        cache = (cache_k[:, l].reshape(dec_batch, -1, kvh * hd),
                 cache_v[:, l].reshape(dec_batch, -1, kvh * hd))
        y_sample, _, _, _ = _trunk(y_sample, sample_rows, mods_head, cache, rope, lw, shared)
    return (y_prompt, y_sample, jnp.stack(new_k, axis=1), jnp.stack(new_v, axis=1))
```

```python
import functools
import math
from typing import NamedTuple

import jax
import jax.numpy as jnp
from jax import lax
from jax.experimental import pallas as pl
from jax.experimental.pallas import tpu as pltpu

F32 = jnp.float32
BF16 = jnp.bfloat16

EPS = 1e-6
GRID_W = 64
ROPE_THETA = 10000.0
KV_GROUP = 4
POOL_WINDOWS = (2, 4, 8, 16)
PEER_TOPK = 16
LANES = 128
SUBLANES = 8
BF16_ROWS = 16
GM_PITCH = 136
EXPERT_ROW_SPLIT = 512
EXPERT_OUT_CHUNK = 512
ATTN_SMALL_SEQ = 256
PROJ_ROWS = 1024
PROJ_COLS = 512
VMEM_LIMIT = 56 * 1024 * 1024


def _params(semantics):
    return pltpu.CompilerParams(dimension_semantics=semantics, vmem_limit_bytes=VMEM_LIMIT)


def _dot(a, b):
    return jnp.dot(a, b, preferred_element_type=F32)


def _dot_nt(a, b):
    return lax.dot_general(a, b, (((1,), (1,)), ((), ())), preferred_element_type=F32)


def _rms(x, g):
    return x * lax.rsqrt(jnp.mean(x * x, axis=-1, keepdims=True) + EPS) * g


def _tile(n, pref):
    return pref if n % pref == 0 else n


def _mods_map(mods, tiles_per_seq):
    if mods.shape[0] == 1:
        return lambda i: (0, 0, 0)
    return lambda i: (i // tiles_per_seq, 0, 0)


def _mods_kernel(c_ref, w_ref, b_ref, o_ref):
    c = c_ref[...]
    s = (c * jax.nn.sigmoid(c)).astype(BF16)
    o_ref[...] = _dot(s, w_ref[...].astype(BF16)) + b_ref[...]


def _mods(cond, w_mod, b_mod, ncols):
    rp, d = cond.shape
    tn = _tile(ncols, 512)
    return pl.pallas_call(
        _mods_kernel,
        grid=(ncols // tn,),
        in_specs=[pl.BlockSpec((rp, d), lambda j: (0, 0)),
                  pl.BlockSpec((d, tn), lambda j: (0, j)),
                  pl.BlockSpec((1, tn), lambda j: (0, j))],
        out_specs=pl.BlockSpec((rp, tn), lambda j: (0, j)),
        out_shape=jax.ShapeDtypeStruct((rp, ncols), F32),
        compiler_params=_params(("arbitrary",)),
        name="mods",
    )(cond, w_mod, b_mod)


class _CastSide(NamedTuple):
    table: jax.Array


class _ModsSide(NamedTuple):
    cond: jax.Array
    w_mod: jax.Array
    b_mod: jax.Array
    col0: int
    ncols: int


_SIDE_INPUTS = {None: 0, _CastSide: 1, _ModsSide: 3}


def _side_kind(side):
    return None if side is None else type(side)


def _side_plumbing(side, nsteps, step):
    if side is None:
        return [], [], [], []
    if isinstance(side, _CastSide):
        rows, width = side.table.shape
        rs = rows // nsteps
        assert rs * nsteps == rows and rs % BF16_ROWS == 0
        blk = pl.BlockSpec((rs, width), lambda *g: (step(*g), 0))
        return [blk], [side.table], [blk], [jax.ShapeDtypeStruct((rows, width), BF16)]
    rp, d = side.cond.shape
    bw = side.ncols // nsteps
    assert bw * nsteps == side.ncols and bw % LANES == 0 and side.col0 % bw == 0
    cb = side.col0 // bw
    in_specs = [pl.BlockSpec((rp, d), lambda *g: (0, 0)),
                pl.BlockSpec((d, bw), lambda *g: (0, cb + step(*g))),
                pl.BlockSpec((1, bw), lambda *g: (0, cb + step(*g)))]
    return (in_specs, [side.cond, side.w_mod, side.b_mod],
            [pl.BlockSpec((rp, bw), lambda *g: (0, step(*g)))],
            [jax.ShapeDtypeStruct((rp, side.ncols), F32)])


def _split_refs(refs, n_in, n_out, side):
    k = _SIDE_INPUTS[side]
    ins, side_in = refs[:n_in], refs[n_in:n_in + k]
    outs = refs[n_in + k:n_in + k + n_out]
    return ins, outs, side_in, (refs[n_in + k + n_out] if side is not None else None)


def _run_side(side, side_in, side_out):
    if side is _CastSide:
        side_out[...] = side_in[0][...].astype(side_out.dtype)
    elif side is _ModsSide:
        _mods_kernel(*side_in, side_out)


def _prenorm_kernel(x_ref, g_ref, m_ref, o_ref):
    y = _rms(x_ref[...], g_ref[...])
    o_ref[...] = (y * (1.0 + m_ref[0, 1:2, :]) + m_ref[0, 0:1, :]).astype(o_ref.dtype)


def _prenorm(x, g, mods, seq):
    n, d = x.shape
    tm = _tile(seq, 256)
    return pl.pallas_call(
        _prenorm_kernel,
        grid=(n // tm,),
        in_specs=[pl.BlockSpec((tm, d), lambda i: (i, 0)),
                  pl.BlockSpec((1, d), lambda i: (0, 0)),
                  pl.BlockSpec((1,) + mods.shape[1:], _mods_map(mods, seq // tm))],
        out_specs=pl.BlockSpec((tm, d), lambda i: (i, 0)),
        out_shape=jax.ShapeDtypeStruct((n, d), BF16),
        compiler_params=_params(("arbitrary",)),
        name="prenorm",
    )(x, g.reshape(1, d), mods)


def _rope(y, cos, sin):
    lane = lax.broadcasted_iota(jnp.int32, y.shape, 1)
    quarter = LANES // 4
    swapped = jnp.where((lane & quarter) == 0,
                        pltpu.roll(y, LANES - quarter, axis=1),
                        pltpu.roll(y, quarter, axis=1))
    return y * cos + swapped * sin


def _proj_kernel(*refs, n_ops, head_norm, rope):
    it = iter(refs)
    ops = [(next(it), next(it)) for _ in range(n_ops)]
    g_ref = next(it) if head_norm else None
    cos_ref, sin_ref = (next(it), next(it)) if rope else (None, None)
    o_ref = next(it)
    acc = None
    for a_ref, w_ref in ops:
        part = _dot(a_ref[...], w_ref[...].astype(BF16))
        acc = part if acc is None else acc + part
    if not head_norm:
        o_ref[...] = acc.astype(o_ref.dtype)
        return
    g = g_ref[...]
    for h in range(acc.shape[1] // LANES):
        cols = slice(h * LANES, (h + 1) * LANES)
        y = _rms(acc[:, cols], g)
        if rope:
            y = _rope(y, cos_ref[...], sin_ref[...])
        o_ref[:, cols] = y.astype(o_ref.dtype)


def _proj(ops, w, col0, ncols, out_dtype, head_g=None, rope=None, seq=None):
    n = ops[0].shape[0]
    tm = _tile(n if rope is None else seq, PROJ_ROWS)
    tn = math.gcd(math.gcd(ncols, col0), PROJ_COLS)
    assert tn % LANES == 0
    jb = col0 // tn
    in_specs, args = [], []
    row0 = 0
    for a in ops:
        k = a.shape[1]
        assert row0 % k == 0
        in_specs += [pl.BlockSpec((tm, k), lambda j, i: (i, 0)),
                     pl.BlockSpec((k, tn), lambda j, i, rb=row0 // k: (rb, jb + j))]
        args += [a, w]
        row0 += k
    if head_g is not None:
        in_specs.append(pl.BlockSpec((1, LANES), lambda j, i: (0, 0)))
        args.append(head_g.reshape(1, LANES))
    if rope is not None:
        per = seq // tm
        for t in rope:
            in_specs.append(pl.BlockSpec((tm, LANES), lambda j, i: (i % per, 0)))
            args.append(t)
    return pl.pallas_call(
        functools.partial(_proj_kernel, n_ops=len(ops), head_norm=head_g is not None, rope=rope is not None),
        grid=(ncols // tn, n // tm),
        in_specs=in_specs,
        out_specs=pl.BlockSpec((tm, tn), lambda j, i: (i, j)),
        out_shape=jax.ShapeDtypeStruct((n, ncols), out_dtype),
        compiler_params=_params(("arbitrary", "arbitrary")),
        name="proj",
    )(*args)


def _attn_kernel(*refs, scale, has_cache, side):
    ins, (o_ref,), side_in, side_out = _split_refs(refs, 5 if has_cache else 3, 1, side)
    _run_side(side, side_in, side_out)
    q_ref, k_ref, v_ref = ins[:3]
    for kv in range(k_ref.shape[1] // LANES):
        kcols = slice(kv * LANES, (kv + 1) * LANES)
        k = k_ref[:, kcols].astype(BF16)
        v = v_ref[:, kcols].astype(BF16)
        if has_cache:
            ck = ins[3][:, kcols].astype(BF16)
            cv = ins[4][:, kcols].astype(BF16)
        for g in range(KV_GROUP):
            cols = slice((kv * KV_GROUP + g) * LANES, (kv * KV_GROUP + g + 1) * LANES)
            qh = q_ref[:, cols]
            s = _dot_nt(qh, k) * scale
            m = jnp.max(s, axis=-1, keepdims=True)
            if has_cache:
                s2 = _dot_nt(qh, ck) * scale
                m = jnp.maximum(m, jnp.max(s2, axis=-1, keepdims=True))
            p = jnp.exp(s - m)
            l = jnp.sum(p, axis=-1, keepdims=True)
            if has_cache:
                p2 = jnp.exp(s2 - m)
                l = l + jnp.sum(p2, axis=-1, keepdims=True)
            inv = 1.0 / l
            o = _dot((p * inv).astype(BF16), v)
            if has_cache:
                o = o + _dot((p2 * inv).astype(BF16), cv)
            o_ref[:, cols] = o.astype(o_ref.dtype)


def _attention(q, k, v, batch, seq, cache=None, side=None):
    n, qw = q.shape
    kvh = k.shape[1] // LANES
    hps = kvh if seq <= ATTN_SMALL_SEQ else 1
    kw = hps * LANES
    gw = KV_GROUP * kw
    tq = _tile(seq, 256)
    per = seq // tq
    nh = kvh // hps
    in_specs = [pl.BlockSpec((tq, gw), lambda b, h, i: (b * per + i, h)),
                pl.BlockSpec((seq, kw), lambda b, h, i: (b, h)),
                pl.BlockSpec((seq, kw), lambda b, h, i: (b, h))]
    args = [q, k, v]
    if cache is not None:
        past = cache[0].shape[1]
        for c in cache:
            in_specs.append(pl.BlockSpec((None, past, kw), lambda b, h, i: (b, 0, h)))
            args.append(c)
    s_in, s_args, s_out, s_shape = _side_plumbing(side, batch * nh * per, lambda b, h, i: (b * nh + h) * per + i)
    outs = pl.pallas_call(
        functools.partial(_attn_kernel, scale=LANES ** -0.5, has_cache=cache is not None, side=_side_kind(side)),
        grid=(batch, nh, per),
        in_specs=in_specs + s_in,
        out_specs=[pl.BlockSpec((tq, gw), lambda b, h, i: (b * per + i, h))] + s_out,
        out_shape=[jax.ShapeDtypeStruct((n, qw), BF16)] + s_shape,
        compiler_params=_params(("arbitrary", "arbitrary", "arbitrary")),
        name="attention",
    )(*args, *s_args)
    return outs[0] if side is None else tuple(outs)


def _pool_kernel(*refs, side):
    (x_ref, w_ref, s_ref), (o_ref,), side_in, side_out = _split_refs(refs, 3, 1, side)
    _run_side(side, side_in, side_out)
    g = pl.program_id(1)
    x = x_ref[...]
    t = x.shape[0]
    row = lax.broadcasted_iota(jnp.int32, x.shape, 0)
    for gi, win in enumerate(POOL_WINDOWS):
        @pl.when(g == gi)
        def _():
            half = win // 2
            acc = x
            for d in range(-half, half):
                if d == 0:
                    continue
                shifted = pltpu.roll(x, (-d) % t, axis=0)
                valid = (row >= -d) if d < 0 else (row < t - d)
                acc = acc + jnp.where(valid, shifted, 0.0)
            cnt = (jnp.minimum(row + half, t) - jnp.maximum(row - half, 0)).astype(F32)
            pooled = acc / cnt - x
            mixed = _dot(pooled.astype(BF16), w_ref[...].astype(BF16)) * s_ref[...]
            o_ref[...] = mixed.astype(o_ref.dtype)


def _pool(xp, w_pool, pool_scale, batch, seq, side=None):
    n, pw = xp.shape
    ng, gd, _ = w_pool.shape
    s_in, s_args, s_out, s_shape = _side_plumbing(side, batch * ng, lambda b, g: b * ng + g)
    outs = pl.pallas_call(
        functools.partial(_pool_kernel, side=_side_kind(side)),
        grid=(batch, ng),
        in_specs=[pl.BlockSpec((seq, gd), lambda b, g: (b, g)),
                  pl.BlockSpec((None, gd, gd), lambda b, g: (g, 0, 0)),
                  pl.BlockSpec((1, gd), lambda b, g: (0, g))] + s_in,
        out_specs=[pl.BlockSpec((seq, gd), lambda b, g: (b, g))] + s_out,
        out_shape=[jax.ShapeDtypeStruct((n, pw), BF16)] + s_shape,
        compiler_params=_params(("arbitrary", "arbitrary")),
        name="pool",
    )(xp, w_pool, pool_scale.reshape(1, pw), *s_args)
    return outs[0] if side is None else tuple(outs)


def _postmix_kernel(x_ref, mix_ref, gpost_ref, gpre_ref, m_ref, x1_ref, h2_ref):
    x1 = x_ref[...] + m_ref[0, 2:3, :] * _rms(mix_ref[...], gpost_ref[...])
    x1_ref[...] = x1
    y = _rms(x1, gpre_ref[...])
    h2_ref[...] = (y * (1.0 + m_ref[0, 4:5, :]) + m_ref[0, 3:4, :]).astype(h2_ref.dtype)


def _postmix(x, mix, g_post, g_pre, mods, seq):
    n, d = x.shape
    tm = _tile(seq, 256)
    row = pl.BlockSpec((tm, d), lambda i: (i, 0))
    vec = pl.BlockSpec((1, d), lambda i: (0, 0))
    return pl.pallas_call(
        _postmix_kernel,
        grid=(n // tm,),
        in_specs=[row, row, vec, vec, pl.BlockSpec((1, 6, d), _mods_map(mods, seq // tm))],
        out_specs=[row, row],
        out_shape=[jax.ShapeDtypeStruct((n, d), F32), jax.ShapeDtypeStruct((n, d), BF16)],
        compiler_params=_params(("arbitrary",)),
        name="postmix",
    )(x, mix, g_post.reshape(1, d), g_pre.reshape(1, d), mods)


def _final_kernel(x_ref, f_ref, g_ref, m_ref, o_ref):
    o_ref[...] = x_ref[...] + m_ref[0, 5:6, :] * _rms(f_ref[...], g_ref[...])


def _final(x1, f, g_post, mods, seq):
    n, d = x1.shape
    tm = _tile(seq, 256)
    row = pl.BlockSpec((tm, d), lambda i: (i, 0))
    return pl.pallas_call(
        _final_kernel,
        grid=(n // tm,),
        in_specs=[row, row, pl.BlockSpec((1, d), lambda i: (0, 0)),
                  pl.BlockSpec((1, 6, d), _mods_map(mods, seq // tm))],
        out_specs=row,
        out_shape=jax.ShapeDtypeStruct((n, d), F32),
        compiler_params=_params(("arbitrary",)),
        name="final",
    )(x1, f, g_post.reshape(1, d), mods)


def _top16(s, ids):
    width = s.shape[1]
    slot = lax.broadcasted_iota(jnp.int32, (PEER_TOPK, width), 0)
    vals = jnp.zeros((PEER_TOPK, width), F32)
    idxs = jnp.zeros((PEER_TOPK, width), F32)
    for k in range(PEER_TOPK):
        m = jnp.max(s, axis=0, keepdims=True)
        idx = jnp.min(jnp.where(s == m, ids, jnp.inf), axis=0, keepdims=True)
        vals = jnp.where(slot == k, m, vals)
        idxs = jnp.where(slot == k, idx, idxs)
        s = jnp.where(ids == idx, -jnp.inf, s)
    return vals, idxs.astype(jnp.int32)


def _row_ids(nrows, width):
    return lax.broadcasted_iota(jnp.int32, (nrows, width), 0).astype(F32)


def _pair_candidates(v1, v2):
    k = PEER_TOPK
    width = v1.shape[1]
    a8 = lax.broadcasted_iota(jnp.int32, (SUBLANES, width), 0)
    vals = [v1 + v2[0:1, :]]
    pos = [_row_ids(k, width) * k]
    for b in range(1, SUBLANES):
        na = k // (b + 1)
        vals.append(jnp.where(a8 < na, v1[0:SUBLANES, :] + v2[b:b + 1, :], -jnp.inf))
        pos.append((a8 * k + b).astype(F32))
    vals.append(v1[0:1, :] + v2[SUBLANES:k, :])
    pos.append((a8 + SUBLANES).astype(F32))
    return jnp.concatenate(vals, axis=0), jnp.concatenate(pos, axis=0)


def _topk_kernel(*refs, side):
    (q_ref, sk_ref), (i_ref, j_ref, g_ref), side_in, side_out = _split_refs(refs, 2, 3, side)
    _run_side(side, side_in, side_out)
    half = sk_ref.shape[-1]
    nkeys = sk_ref.shape[-2]
    tt = q_ref.shape[0]
    ids = _row_ids(nkeys, tt)
    v1, i1 = _top16(_dot_nt(sk_ref[0].astype(BF16), q_ref[:, :half]), ids)
    v2, i2 = _top16(_dot_nt(sk_ref[1].astype(BF16), q_ref[:, half:]), ids)
    tv, pos = _top16(*_pair_candidates(v1, v2))
    a_sel = pos // PEER_TOPK
    b_sel = pos % PEER_TOPK
    isel = jnp.zeros_like(pos)
    jsel = jnp.zeros_like(pos)
    for a in range(PEER_TOPK):
        isel = jnp.where(a_sel == a, i1[a:a + 1, :], isel)
        jsel = jnp.where(b_sel == a, i2[a:a + 1, :], jsel)
    e = jnp.exp(tv - tv[0:1, :])
    i_ref[...] = isel
    j_ref[...] = jsel
    g_ref[...] = e / jnp.sum(e, axis=0, keepdims=True)


def _peer_topk(q, sub_keys, side=None):
    n = q.shape[0]
    heads, _, nkeys, half = sub_keys.shape
    tt = _tile(n, 512)
    out = pl.BlockSpec((None, PEER_TOPK, tt), lambda i, h: (h, 0, i))
    s_in, s_args, s_out, s_shape = _side_plumbing(side, (n // tt) * heads, lambda i, h: i * heads + h)
    return pl.pallas_call(
        functools.partial(_topk_kernel, side=_side_kind(side)),
        grid=(n // tt, heads),
        in_specs=[pl.BlockSpec((tt, 2 * half), lambda i, h: (i, h)),
                  pl.BlockSpec((None, 2, nkeys, half), lambda i, h: (h, 0, 0, 0))] + s_in,
        out_specs=[out, out, out] + s_out,
        out_shape=[jax.ShapeDtypeStruct((heads, PEER_TOPK, n), jnp.int32),
                   jax.ShapeDtypeStruct((heads, PEER_TOPK, n), jnp.int32),
                   jax.ShapeDtypeStruct((heads, PEER_TOPK, n), F32)] + s_shape,
        compiler_params=_params(("arbitrary", "arbitrary")),
        name="peer_topk",
    )(q, sub_keys, *s_args)


def _gate_matrix_kernel(i_ref, j_ref, g_ref, o_ref, t_even_ref, t_odd_ref, *, nkeys):
    ngroups = i_ref.shape[0] // BF16_ROWS
    assert ngroups % 2 == 0
    sub = lax.broadcasted_iota(jnp.int32, (nkeys, i_ref.shape[1]), 0)

    def first_row(gidx):
        return gidx * BF16_ROWS if isinstance(gidx, int) else pl.multiple_of(gidx * BF16_ROWS, BF16_ROWS)

    def build(gidx, t_ref):
        base = first_row(gidx)
        for r in range(BF16_ROWS):
            irow = i_ref[pl.ds(base + r, 1), :]
            jrow = j_ref[pl.ds(base + r, 1), :]
            grow = g_ref[pl.ds(base + r, 1), :]
            a = jnp.where(sub == irow, grow, 0.0).astype(BF16)
            b = jnp.where(sub == jrow, 1.0, 0.0).astype(BF16)
            t_ref[r * GM_PITCH:r * GM_PITCH + nkeys, :] = _dot_nt(a, b)

    def emit(gidx, t_ref):
        base = first_row(gidx)
        for i in range(nkeys):
            rows = t_ref[pl.ds(i, BF16_ROWS, stride=GM_PITCH), :]
            o_ref[pl.ds(base, BF16_ROWS), i * nkeys:(i + 1) * nkeys] = rows.astype(o_ref.dtype)

    def pair(m, carry):
        build(2 * m + 1, t_odd_ref)
        emit(2 * m, t_even_ref)
        build(jnp.minimum(2 * m + 2, ngroups - 1), t_even_ref)
        emit(2 * m + 1, t_odd_ref)
        return carry

    build(0, t_even_ref)
    lax.fori_loop(0, ngroups // 2, pair, 0)


def _gate_matrix(isel, jsel, gate, nkeys):
    n, s = isel.shape
    assert nkeys == LANES
    tile = _tile(n, 256)
    row = pl.BlockSpec((tile, s), lambda i: (i, 0))
    scratch = pltpu.VMEM((BF16_ROWS * GM_PITCH, nkeys), F32)
    return pl.pallas_call(
        functools.partial(_gate_matrix_kernel, nkeys=nkeys),
        grid=(n // tile,),
        in_specs=[row, row, row],
        out_specs=pl.BlockSpec((tile, nkeys * nkeys), lambda i: (i, 0)),
        out_shape=jax.ShapeDtypeStruct((n, nkeys * nkeys), BF16),
        scratch_shapes=[scratch, scratch],
        compiler_params=_params(("arbitrary",)),
        name="gate_matrix",
    )(isel, jsel, gate)


def _gelu_tanh(x):
    return x * (0.5 * (1.0 + jnp.tanh(math.sqrt(2.0 / math.pi) * (x + 0.044715 * (x * x * x)))))


def _expert_kernel(h_ref, gm_ref, u_ref, v_ref, o_ref):
    @pl.when(pl.program_id(1) == 0)
    def _():
        o_ref[...] = jnp.zeros_like(o_ref)

    tb, d = o_ref.shape
    half = min(tb, EXPERT_ROW_SPLIT)
    chunk = min(d, EXPERT_OUT_CHUNK)
    for r in range(0, tb, half):
        rows = slice(r, r + half)
        act = _gelu_tanh(_dot_nt(h_ref[rows, :], u_ref[...]))
        w = (gm_ref[rows, :].astype(F32) * act).astype(BF16)
        for c in range(0, d, chunk):
            o_ref[rows, c:c + chunk] += _dot(w, v_ref[:, c:c + chunk])


def _experts(h, gm, u, v):
    n, d = h.shape
    ne = u.shape[0]
    tb = _tile(n, 1024)
    te = _tile(ne, 512)
    once = pl.Buffered(1)
    return pl.pallas_call(
        _expert_kernel,
        grid=(n // tb, ne // te),
        in_specs=[pl.BlockSpec((tb, d), lambda i, e: (i, 0), pipeline_mode=once),
                  pl.BlockSpec((tb, te), lambda i, e: (i, e)),
                  pl.BlockSpec((te, d), lambda i, e: (e, 0)),
                  pl.BlockSpec((te, d), lambda i, e: (e, 0))],
        out_specs=pl.BlockSpec((tb, d), lambda i, e: (i, 0), pipeline_mode=once),
        out_shape=jax.ShapeDtypeStruct((n, d), F32),
        compiler_params=_params(("arbitrary", "arbitrary")),
        name="experts",
    )(h, gm, u, v)


def _rope_tables(seq):
    pairs = LANES // 4
    t = jnp.arange(seq)
    inv = ROPE_THETA ** (-jnp.arange(pairs, dtype=F32) / pairs)
    ang_r = (t // GRID_W).astype(F32)[:, None] * inv
    ang_c = (t % GRID_W).astype(F32)[:, None] * inv
    cos = jnp.concatenate([jnp.cos(ang_r)] * 2 + [jnp.cos(ang_c)] * 2, axis=-1)
    sin = jnp.concatenate([-jnp.sin(ang_r), jnp.sin(ang_r), -jnp.sin(ang_c), jnp.sin(ang_c)], axis=-1)
    return cos, sin


class _LayerWeights(NamedTuple):
    pre_mix_g: jax.Array
    post_mix_g: jax.Array
    pre_ffn_g: jax.Array
    post_ffn_g: jax.Array
    w_in: jax.Array
    q_norm_g: jax.Array
    k_norm_g: jax.Array
    w_pool: jax.Array
    pool_scale: jax.Array
    w_out: jax.Array
    peer_w_query: jax.Array
    peer_sub_keys: jax.Array


class _Host(NamedTuple):
    cond: jax.Array
    w_mod: jax.Array
    b_mod: jax.Array
    peer_u: jax.Array
    peer_v: jax.Array


class _Front(NamedTuple):
    x1: jax.Array
    h2: jax.Array
    gm: jax.Array
    k: jax.Array
    v: jax.Array


def _trunk_front(x, rows, mods_head, mods, cache, rope, lw, host, host_u, host_v):
    batch, seq, d = x.shape
    n = batch * seq
    x2 = x.reshape(n, d)
    attn_w = d // 2
    kv_w = attn_w // KV_GROUP
    host_mods = mods is None
    mods_side = lambda part: _ModsSide(host.cond, host.w_mod, host.b_mod, (1 + part) * 2 * d, 2 * d)

    h = _prenorm(x2, lw.pre_mix_g, mods_head[rows], seq)
    q = _proj([h], lw.w_in, 0, attn_w, BF16, head_g=lw.q_norm_g, rope=rope, seq=seq)
    k = _proj([h], lw.w_in, attn_w, kv_w, F32, head_g=lw.k_norm_g, rope=rope, seq=seq)
    v = _proj([h], lw.w_in, attn_w + kv_w, kv_w, F32)
    xp = _proj([h], lw.w_in, attn_w + 2 * kv_w, d - attn_w, F32)
    attn_side = mods_side(1) if host_mods else (_CastSide(host.peer_v) if host_v else None)
    attn = _attention(q, k, v, batch, seq, cache, side=attn_side)
    pool = _pool(xp, lw.w_pool, lw.pool_scale, batch, seq, side=mods_side(0) if host_mods else None)
    peer_v = None
    if host_mods:
        (attn, mods_tail), (pool, mods_mid) = attn, pool
        nrows = mods_head.shape[0]
        mods = jnp.concatenate([mods_head.reshape(nrows, 2 * d), mods_mid[:nrows], mods_tail[:nrows]], axis=1)
        mods = mods.reshape(nrows, 6, d)
    elif host_v:
        attn, peer_v = attn
    mix = _proj([attn, pool], lw.w_out, 0, d, F32)
    x1, h2 = _postmix(x2, mix, lw.post_mix_g, lw.pre_ffn_g, mods[rows], seq)

    heads, _, nkeys, _ = lw.peer_sub_keys.shape
    pq = _proj([h2], lw.peer_w_query, 0, lw.peer_w_query.shape[1], BF16)
    isel, jsel, gate, *cast = _peer_topk(pq, lw.peer_sub_keys, side=_CastSide(host.peer_u) if host_u else None)
    flat = lambda a: a.reshape(heads * PEER_TOPK, n).T
    gm = _gate_matrix(flat(isel), flat(jsel), flat(gate), nkeys)
    return _Front(x1, h2, gm, k, v), mods, (cast[0] if host_u else None), peer_v


def _trunk_back(front, mods, peer_u, peer_v, lw, shape):
    batch, seq, d = shape
    f = _experts(front.h2, front.gm, peer_u, peer_v)
    return _final(front.x1, f, lw.post_ffn_g, mods, seq).reshape(batch, seq, d)


def kernel(x_prompt, x_sample, cache_k, cache_v, c, c_ctx, w_mod, b_mod, pre_mix_g, post_mix_g,
           pre_ffn_g, post_ffn_g, w_in, q_norm_g, k_norm_g, w_pool, pool_scale, w_out,
           peer_w_query, peer_sub_keys, peer_u, peer_v):
    depth, d, _ = w_mod.shape
    dec_batch, dec_seq, _ = x_sample.shape
    batch, seq, _ = x_prompt.shape
    kvh, hd = cache_k.shape[3], cache_k.shape[4]
    rope = _rope_tables(dec_seq)
    cond = jnp.concatenate([c_ctx[None, :], c], axis=0)
    nrows = cond.shape[0]
    cond = jnp.zeros((SUBLANES, d), F32).at[:nrows].set(cond)
    y_prompt, y_sample = x_prompt, x_sample
    new_k, new_v = [], []
    for l in range(depth):
        lw = _LayerWeights(*(p[l] for p in (pre_mix_g, post_mix_g, pre_ffn_g, post_ffn_g, w_in, q_norm_g,
                                            k_norm_g, w_pool, pool_scale, w_out, peer_w_query,
                                            peer_sub_keys)))
        b_l = b_mod[l].reshape(1, -1)
        mods_head = _mods(cond, w_mod[l], b_l, 2 * d).reshape(SUBLANES, 2, d)[:nrows]
        host = _Host(cond, w_mod[l], b_l, peer_u[l], peer_v[l])
        rows_p, rows_s = slice(0, 1), slice(1, nrows)
        front_p, mods, u_bf16, _ = _trunk_front(y_prompt, rows_p, mods_head, None, None, None, lw, host,
                                                True, False)
        cache = (cache_k[:, l].reshape(dec_batch, -1, kvh * hd),
                 cache_v[:, l].reshape(dec_batch, -1, kvh * hd))
        front_s, _, _, v_bf16 = _trunk_front(y_sample, rows_s, mods_head, mods, cache, rope, lw, host,
                                             False, True)
        y_prompt = _trunk_back(front_p, mods[rows_p], u_bf16, v_bf16, lw, y_prompt.shape)
        y_sample = _trunk_back(front_s, mods[rows_s], u_bf16, v_bf16, lw, y_sample.shape)
        new_k.append(front_p.k.reshape(batch, seq, kvh, hd))
        new_v.append(front_p.v.reshape(batch, seq, kvh, hd))
    return (y_prompt, y_sample, jnp.stack(new_k, axis=1), jnp.stack(new_v, axis=1))
```
